```python
import jax, jax.numpy as jnp
from jax import lax
import numpy as np

D_MODEL = 2048
BATCH = 16
SEQ = 256
DEPTH = 2
DEC_BATCH = 2
DEC_SEQ = 4096
PAST_LEN = 512

GRID_W = 64
CHUNK = 64
HEAD_DIM = 128
H_A = 8
H_B = 8
H_C = 8
W_MIX = 8 * HEAD_DIM
CONV_K = 3
D_FF = 4096
N_MOD = 9
ROPE_BASE = 10000.0
EPS = 1e-6

IN_SIZES = [W_MIX, W_MIX, W_MIX, W_MIX, 2 * H_A, 2 * H_A,
            3 * W_MIX, W_MIX, 2 * H_B, 2 * H_B,
            W_MIX, W_MIX, W_MIX, W_MIX,
            3 * D_MODEL]
N_IN = int(sum(IN_SIZES))
IN_OFFSETS = [int(o) for o in np.cumsum(IN_SIZES)[:-1]]

kernel_name = "hybrid_mlstm_deltanet_retention_flow_step"


def rmsnorm(x, g):
    xf = x.astype(jnp.float32)
    y = xf * lax.rsqrt(jnp.mean(xf * xf, axis=-1, keepdims=True) + EPS)
    return (y * g.astype(jnp.float32)).astype(x.dtype)


def head_rms(x, g):
    y = x * lax.rsqrt(jnp.mean(x * x, axis=-1, keepdims=True) + EPS)
    return y * g.astype(jnp.float32)


def l2norm(x):
    return x * lax.rsqrt(jnp.sum(x * x, axis=-1, keepdims=True) + EPS)


def heads(x, h):
    return x.reshape(*x.shape[:-1], h, -1)


def flip(x):
    return jnp.flip(x, axis=1)


def chunk(x):
    b, t, h, d = x.shape
    return x.reshape(b, t // CHUNK, CHUNK, h, d).transpose(1, 0, 3, 2, 4)


def unchunk(y):
    nc, b, h, l, d = y.shape
    return y.transpose(1, 0, 3, 2, 4).reshape(b, nc * l, h, d)


def masks():
    idx = jnp.arange(CHUNK)
    return idx[:, None] >= idx[None, :], idx[:, None] > idx[None, :]


def mlstm_scan(q, k, v, ig, lf, C0, n0, m0):
    qc, kc, vc = chunk(q), chunk(k), chunk(v)
    igc = chunk(ig[..., None])[..., 0]
    b = jnp.cumsum(chunk(lf[..., None])[..., 0], axis=-1)
    incl, _ = masks()
    logD = jnp.where(incl, b[..., :, None] - b[..., None, :] + igc[..., None, :], -jnp.inf)
    m_in = jnp.max(logD, axis=-1)
    bL = b[..., -1]
    log_end = bL[..., None] - b + igc
    m_end = jnp.max(log_end, axis=-1)
    qk = jnp.einsum('nbhld,nbhmd->nbhlm', qc, kc)

    def step(carry, inp):
        C, n, m = carry
        qn, kn, vn, bn, logDn, m_inn, qkn, le_n, me_n, bL_n = inp
        m_i = jnp.maximum(bn + m[..., None], m_inn)
        w_prev = jnp.exp(bn + m[..., None] - m_i)
        S = qkn * jnp.exp(logDn - m_i[..., None])
        num = w_prev[..., None] * jnp.einsum('bhld,bhde->bhle', qn, C) + jnp.einsum('bhlm,bhme->bhle', S, vn)
        den = w_prev * jnp.einsum('bhld,bhd->bhl', qn, n) + jnp.sum(S, axis=-1)
        h = num / jnp.maximum(jnp.abs(den), jnp.exp(-m_i))[..., None]
        m_new = jnp.maximum(bL_n + m, me_n)
        w_s = jnp.exp(bL_n + m - m_new)
        w_t = jnp.exp(le_n - m_new[..., None])
        C = w_s[..., None, None] * C + jnp.einsum('bhld,bhle->bhde', kn * w_t[..., None], vn)
        n = w_s[..., None] * n + jnp.einsum('bhld,bhl->bhd', kn, w_t)
        return (C, n, m_new), h

    (C, n, m), h = lax.scan(step, (C0, n0, m0), (qc, kc, vc, b, logD, m_in, qk, log_end, m_end, bL))
    return unchunk(h), C, n, m


def delta_scan(q, k, v, g, beta, S0):
    qc, kc, vc = chunk(q), chunk(k), chunk(v)
    G = jnp.cumsum(chunk(g[..., None])[..., 0], axis=-1)
    bc = chunk(beta[..., None])[..., 0]
    incl, strict = masks()
    diff = G[..., :, None] - G[..., None, :]
    decay = jnp.where(incl, jnp.exp(jnp.where(incl, diff, 0.0)), 0.0)
    A = jnp.where(strict, jnp.einsum('nbhld,nbhmd->nbhlm', kc, kc) * decay, 0.0) * bc[..., :, None]
    IA = A + jnp.eye(CHUNK, dtype=A.dtype)
    rhs = jnp.concatenate([bc[..., None] * vc, (bc * jnp.exp(G))[..., None] * kc], axis=-1)
    sol = lax.linalg.triangular_solve(IA, rhs, left_side=True, lower=True, unit_diagonal=True)
    U0, Wk = sol[..., :HEAD_DIM], sol[..., HEAD_DIM:]
    qk = jnp.einsum('nbhld,nbhmd->nbhlm', qc, kc) * decay
    q_dec = qc * jnp.exp(G)[..., None]
    k_end = kc * jnp.exp(G[..., -1:] - G)[..., None]
    c_dec = jnp.exp(G[..., -1])

    def step(S, inp):
        qk_n, qd_n, ke_n, U0_n, W_n, cd_n = inp
        U = U0_n - jnp.einsum('bhld,bhde->bhle', W_n, S)
        o = jnp.einsum('bhld,bhde->bhle', qd_n, S) + jnp.einsum('bhlm,bhme->bhle', qk_n, U)
        S = cd_n[..., None, None] * S + jnp.einsum('bhld,bhle->bhde', ke_n, U)
        return S, o

    S_fin, o = lax.scan(step, S0, (qk, q_dec, k_end, U0, Wk, c_dec))
    return unchunk(o), S_fin


def retention_scan(q, k, v, log_gamma, S0):
    qc, kc, vc = chunk(q), chunk(k), chunk(v)
    idx = jnp.arange(CHUNK, dtype=jnp.float32)
    incl, _ = masks()
    lg = log_gamma.astype(jnp.float32)[:, None]
    decay = jnp.where(incl, jnp.exp(lg[..., None] * jnp.maximum(idx[:, None] - idx[None, :], 0.0)), 0.0)
    inner = jnp.einsum('nbhlm,nbhme->nbhle', jnp.einsum('nbhld,nbhmd->nbhlm', qc, kc) * decay, vc)
    q_dec = qc * jnp.exp(lg * (idx + 1.0))[..., None]
    k_end = kc * jnp.exp(lg * (CHUNK - 1.0 - idx))[..., None]
    c_dec = jnp.exp(lg * CHUNK)[..., None]

    def step(S, inp):
        qd, ke, vn = inp
        o = jnp.einsum('bhld,bhde->bhle', qd, S)
        S = c_dec * S + jnp.einsum('bhld,bhle->bhde', ke, vn)
        return S, o

    S_fin, cross = lax.scan(step, S0, (q_dec, k_end, vc))
    return unchunk(inner + cross), S_fin


def short_conv(x, w):
    ch = x.shape[-1]
    return lax.conv_general_dilated(x, w.astype(x.dtype)[:, None, :], window_strides=(1,),
                                    padding=[(CONV_K // 2, CONV_K // 2)],
                                    dimension_numbers=('NWC', 'WIO', 'NWC'), feature_group_count=ch)


def rope_tables(T):
    n_rows = T // GRID_W
    pos_r = jnp.repeat(jnp.arange(n_rows), GRID_W).astype(jnp.float32)
    pos_c = jnp.tile(jnp.arange(GRID_W), n_rows).astype(jnp.float32)
    nf = HEAD_DIM // 4
    freqs = ROPE_BASE ** (-jnp.arange(nf, dtype=jnp.float32) / nf)
    ang = jnp.concatenate([pos_r[:, None] * freqs, pos_c[:, None] * freqs], axis=-1)
    ang = jnp.concatenate([ang, ang], axis=-1)[:, None, :]
    return jnp.cos(ang), jnp.sin(ang)


def apply_rope(x, cs):
    cos, sin = cs
    x1, x2 = x[..., :HEAD_DIM // 2], x[..., HEAD_DIM // 2:]
    return x * cos + jnp.concatenate([-x2, x1], axis=-1) * sin


def swiglu(h, w13, w2):
    a, b = jnp.split(h @ w13, 2, axis=-1)
    return (jax.nn.silu(a) * b) @ w2


def token_mixing(h, states, rope, w_in, b_in, f_bias, conv_w, A_log, dt_bias, log_gamma, hn_g, w_br, w_out):
    f32 = jnp.float32
    Bsz, T, _ = h.shape
    P = (h @ w_in + b_in).astype(f32)
    qA, kA, vA, oA, iA, fA, qkvB, zB, betaB, aB, qC, kC, vC, gC, gm = jnp.split(P, IN_OFFSETS, axis=-1)
    C0, n0, m0, SD0, SR0 = [s.astype(f32) for s in states]
    qA = heads(qA, H_A)
    kA = heads(kA, H_A) * HEAD_DIM ** -0.5
    vA = heads(vA, H_A)
    iA = iA.reshape(Bsz, T, 2, H_A)
    lfA = jax.nn.log_sigmoid(fA.reshape(Bsz, T, 2, H_A) + f_bias.astype(f32))
    hf, Cf, nf, mf = mlstm_scan(qA, kA, vA, iA[:, :, 0], lfA[:, :, 0], C0[:, 0], n0[:, 0], m0[:, 0])
    hb, Cb, nb, mb = mlstm_scan(flip(qA), flip(kA), flip(vA), flip(iA[:, :, 1]), flip(lfA[:, :, 1]),
                                C0[:, 1], n0[:, 1], m0[:, 1])
    yA = head_rms(hf + flip(hb), hn_g[0]) * jax.nn.sigmoid(heads(oA, H_A))
    qkvB = jax.nn.silu(short_conv(qkvB, conv_w))
    qB, kB, vB = jnp.split(qkvB, 3, axis=-1)
    qB = l2norm(heads(qB, H_B)) * HEAD_DIM ** -0.5
    kB = l2norm(heads(kB, H_B))
    vB = heads(vB, H_B)
    beta = jax.nn.sigmoid(betaB.reshape(Bsz, T, 2, H_B))
    gB = -jnp.exp(A_log.astype(f32)) * jax.nn.softplus(aB.reshape(Bsz, T, 2, H_B) + dt_bias.astype(f32))
    of, SDf = delta_scan(qB, kB, vB, gB[:, :, 0], beta[:, :, 0], SD0[:, 0])
    ob, SDb = delta_scan(flip(qB), flip(kB), flip(vB), flip(gB[:, :, 1]), flip(beta[:, :, 1]), SD0[:, 1])
    yB = head_rms(of + flip(ob), hn_g[1]) * jax.nn.silu(heads(zB, H_B))
    qC = heads(qC, H_C) * HEAD_DIM ** -0.5
    kC = heads(kC, H_C)
    vC = heads(vC, H_C)
    if rope is not None:
        qC = apply_rope(qC, rope)
        kC = apply_rope(kC, rope)
    rf, SRf = retention_scan(qC, kC, vC, log_gamma[0], SR0[:, 0])
    rb, SRb = retention_scan(flip(qC), flip(kC), flip(vC), log_gamma[1], SR0[:, 1])
    yC = head_rms(rf + flip(rb), hn_g[2]) * jax.nn.silu(heads(gC, H_C))
    ys = jnp.stack([yA.reshape(Bsz, T, W_MIX), yB.reshape(Bsz, T, W_MIX), yC.reshape(Bsz, T, W_MIX)], axis=2).astype(h.dtype)
    br = jnp.einsum('btiw,iwd->btid', ys, w_br)
    gates = jax.nn.sigmoid(gm.reshape(Bsz, T, 3, -1)).astype(h.dtype)
    out = jnp.sum(gates * br, axis=2) @ w_out
    new_states = (jnp.stack([Cf, Cb], axis=1), jnp.stack([nf, nb], axis=1), jnp.stack([mf, mb], axis=1),
                  jnp.stack([SDf, SDb], axis=1), jnp.stack([SRf, SRb], axis=1))
    return out, new_states


def adaln(cv, w, b):
    return (jax.nn.silu(cv) @ w + b).reshape(cv.shape[0], N_MOD, -1)


def trunk_layer(x, ada, states, rope, norm_g, w_in, b_in, f_bias, conv_w, A_log, dt_bias, log_gamma,
                hn_g, w_br, w_out, w13, w2):
    mod = [ada[:, i][:, None, :] for i in range(N_MOD)]
    h = rmsnorm(x, norm_g[0]) * (1.0 + mod[1]) + mod[0]
    x = x + 0.5 * mod[2] * swiglu(h, w13[0], w2[0])
    h = rmsnorm(x, norm_g[1]) * (1.0 + mod[4]) + mod[3]
    out, new_states = token_mixing(h, states, rope, w_in, b_in, f_bias, conv_w, A_log, dt_bias, log_gamma,
                                   hn_g, w_br, w_out)
    x = x + mod[5] * out
    h = rmsnorm(x, norm_g[2]) * (1.0 + mod[7]) + mod[6]
    x = x + 0.5 * mod[8] * swiglu(h, w13[1], w2[1])
    return x, new_states


def setup_inputs(seed: int = 0) -> dict:
    key = jax.random.key(seed)
    ks = jax.random.split(key, 32)
    nrm = jax.random.normal
    D = D_MODEL
    sd = D ** -0.5
    f_bias = jnp.linspace(3.0, 6.0, H_A)[None, None, :] + 0.1 * nrm(ks[13], (DEPTH, 2, H_A))
    dt = jnp.exp(jax.random.uniform(ks[16], (DEPTH, 2, H_B), minval=np.log(0.001), maxval=np.log(0.1)))
    base_lg = jnp.log(1.0 - 2.0 ** (-5.0 - jnp.arange(H_C, dtype=jnp.float32)))
    return {
        'x_prompt': nrm(ks[0], (BATCH, SEQ, D)),
        'x_sample': nrm(ks[1], (DEC_BATCH, DEC_SEQ, D)),
        'state_mlstm_C': 0.1 * nrm(ks[2], (DEC_BATCH, DEPTH, 2, H_A, HEAD_DIM, HEAD_DIM)),
        'state_mlstm_n': 0.1 * nrm(ks[3], (DEC_BATCH, DEPTH, 2, H_A, HEAD_DIM)),
        'state_mlstm_m': nrm(ks[4], (DEC_BATCH, DEPTH, 2, H_A)),
        'state_delta_S': 0.1 * nrm(ks[5], (DEC_BATCH, DEPTH, 2, H_B, HEAD_DIM, HEAD_DIM)),
        'state_ret_S': 0.5 * nrm(ks[6], (DEC_BATCH, DEPTH, 2, H_C, HEAD_DIM, HEAD_DIM)),
        'c': nrm(ks[7], (DEC_BATCH, D)),
        'c_ctx': nrm(ks[8], (D,)),
        'norm_g': 1.0 + 0.02 * nrm(ks[9], (DEPTH, 3, D)),
        'final_norm_g': 1.0 + 0.02 * nrm(ks[10], (D,)),
        'w_ada': 0.5 * sd * nrm(ks[11], (DEPTH, D, N_MOD * D)),
        'b_ada': 0.02 * nrm(ks[12], (DEPTH, N_MOD * D)),
        'w_in': sd * nrm(ks[14], (DEPTH, D, N_IN)),
        'b_in': 0.02 * nrm(ks[15], (DEPTH, N_IN)),
        'mlstm_f_bias': f_bias,
        'conv_w': CONV_K ** -0.5 * nrm(ks[17], (DEPTH, CONV_K, 3 * W_MIX)),
        'delta_A_log': jnp.log(jax.random.uniform(ks[18], (DEPTH, 2, H_B), minval=1.0, maxval=16.0)),
        'delta_dt_bias': dt + jnp.log(-jnp.expm1(-dt)),
        'ret_log_gamma': base_lg[None, None, :] * (1.0 + 0.05 * nrm(ks[19], (DEPTH, 2, H_C))),
        'head_norm_g': 1.0 + 0.02 * nrm(ks[20], (DEPTH, 3, HEAD_DIM)),
        'w_br': W_MIX ** -0.5 * nrm(ks[21], (DEPTH, 3, W_MIX, D)),
        'w_out': sd * nrm(ks[22], (DEPTH, D, D)),
        'ffn_w13': sd * nrm(ks[23], (DEPTH, 2, D, 2 * D_FF)),
        'ffn_w2': D_FF ** -0.5 * nrm(ks[24], (DEPTH, 2, D_FF, D)),
    }


def reference(x_prompt, x_sample, state_mlstm_C, state_mlstm_n, state_mlstm_m, state_delta_S, state_ret_S,
              c, c_ctx, norm_g, final_norm_g, w_ada, b_ada, w_in, b_in, mlstm_f_bias, conv_w, delta_A_log,
              delta_dt_bias, ret_log_gamma, head_norm_g, w_br, w_out, ffn_w13, ffn_w2):
    f32 = jnp.float32
    Bp = x_prompt.shape[0]
    zero_states = (jnp.zeros((Bp, 2, H_A, HEAD_DIM, HEAD_DIM), f32), jnp.zeros((Bp, 2, H_A, HEAD_DIM), f32),
                   jnp.zeros((Bp, 2, H_A), f32), jnp.zeros((Bp, 2, H_B, HEAD_DIM, HEAD_DIM), f32),
                   jnp.zeros((Bp, 2, H_C, HEAD_DIM, HEAD_DIM), f32))
    rope = rope_tables(x_sample.shape[1])
    xp, xs = x_prompt, x_sample
    ctx_states = []
    for l in range(DEPTH):
        lw = (norm_g[l], w_in[l], b_in[l], mlstm_f_bias[l], conv_w[l], delta_A_log[l], delta_dt_bias[l],
              ret_log_gamma[l], head_norm_g[l], w_br[l], w_out[l], ffn_w13[l], ffn_w2[l])
        xp, st = trunk_layer(xp, adaln(c_ctx[None, :], w_ada[l], b_ada[l]), zero_states, None, *lw)
        ctx_states.append(st)
        cache_l = (state_mlstm_C[:, l], state_mlstm_n[:, l], state_mlstm_m[:, l], state_delta_S[:, l], state_ret_S[:, l])
        xs, _ = trunk_layer(xs, adaln(c, w_ada[l], b_ada[l]), cache_l, rope, *lw)
    y_prompt = rmsnorm(xp, final_norm_g)
    y_sample = rmsnorm(xs, final_norm_g)
    dt_out = x_prompt.dtype
    new_mlstm_C = jnp.stack([s[0] for s in ctx_states], axis=1).astype(dt_out)
    new_mlstm_n = jnp.stack([s[1] for s in ctx_states], axis=1).astype(dt_out)
    new_mlstm_m = jnp.stack([s[2] for s in ctx_states], axis=1).astype(dt_out)
    new_delta_S = jnp.stack([s[3] for s in ctx_states], axis=1).astype(dt_out)
    new_ret_S = jnp.stack([s[4] for s in ctx_states], axis=1).astype(dt_out)
    return (y_prompt, y_sample, new_mlstm_C, new_mlstm_n, new_mlstm_m, new_delta_S, new_ret_S)
```

```python
import functools

import jax
import jax.numpy as jnp
import numpy as np
from jax import lax
from jax.experimental import pallas as pl
from jax.experimental.pallas import tpu as pltpu

F32 = jnp.float32
BF16 = jnp.bfloat16

D_MODEL = 2048
DEPTH = 2
N_HEADS = 8
HEAD_DIM = 128
W_MIX = N_HEADS * HEAD_DIM
D_FF = 4096
N_MOD = 9
CONV_K = 3
GRID_W = 64
ROPE_BASE = 10000.0
EPS = 1e-6
ROW_GROUP = 4096
N_MAIN = 12 * W_MIX + 3 * D_MODEL
GATE_LANES = 128
CHUNK_A = 128
CHUNK_B = 64
CHUNK_C = 128
NEG_BIG = -1e30
VMEM_LIMIT = 56 * 1024 * 1024


def _params(sem):
    return pltpu.CompilerParams(dimension_semantics=sem, vmem_limit_bytes=VMEM_LIMIT)


def _dot(a, b):
    return jnp.dot(a, b, preferred_element_type=F32)


def _dot_nt(a, b):
    return lax.dot_general(a, b, (((1,), (1,)), ((), ())), preferred_element_type=F32)


def _dot_tn(a, b):
    return lax.dot_general(a, b, (((0,), (0,)), ((), ())), preferred_element_type=F32)


def _split3(x):
    hi = x.astype(BF16)
    r1 = x - hi.astype(F32)
    mid = r1.astype(BF16)
    lo = (r1 - mid.astype(F32)).astype(BF16)
    return hi, mid, lo


def _dot_exact_rhs01(x, sel):
    hi, mid, lo = _split3(x)
    return _dot(hi, sel) + _dot(mid, sel) + _dot(lo, sel)


def _dot_x3(a, b):
    ah = a.astype(BF16)
    al = (a - ah.astype(F32)).astype(BF16)
    bh = b.astype(BF16)
    bl = (b - bh.astype(F32)).astype(BF16)
    return _dot(ah, bh) + _dot(ah, bl) + _dot(al, bh)


def _sigmoid(x):
    return 1.0 / (1.0 + jnp.exp(-x))


def _softplus(x):
    return jnp.maximum(x, 0.0) + jnp.log(1.0 + jnp.exp(-jnp.abs(x)))


def _ada_kernel(cv_ref, w_ref, b_ref, o_ref):
    cv = cv_ref[...]
    s = (cv * _sigmoid(cv)).astype(BF16)
    o_ref[...] = _dot(s, w_ref[...].astype(BF16)) + b_ref[...]


def _ada(cv8, w_ada, b_ada):
    n = w_ada.shape[-1]
    bn = 1024
    return pl.pallas_call(
        _ada_kernel,
        grid=(DEPTH, n // bn),
        in_specs=[pl.BlockSpec((8, D_MODEL), lambda l, j: (0, 0)),
                  pl.BlockSpec((None, D_MODEL, bn), lambda l, j: (l, 0, j)),
                  pl.BlockSpec((None, 1, bn), lambda l, j: (l, 0, j))],
        out_specs=pl.BlockSpec((None, 8, bn), lambda l, j: (l, 0, j)),
        out_shape=jax.ShapeDtypeStruct((DEPTH, 8, n), F32),
        compiler_params=_params(("arbitrary", "arbitrary")),
        name="adaln",
    )(cv8, w_ada, b_ada.reshape(DEPTH, 1, n))


def _normmod_kernel(x_ref, g_ref, sh_ref, sc_ref, o_ref):
    x = x_ref[...]
    y = x * lax.rsqrt(jnp.mean(x * x, axis=-1, keepdims=True) + EPS) * g_ref[...]
    o_ref[...] = (y * (1.0 + sc_ref[...]) + sh_ref[...]).astype(o_ref.dtype)


def _normmod(x, g, shift, scale):
    m = x.shape[0]
    bm = 256
    grp = ROW_GROUP // bm
    return pl.pallas_call(
        _normmod_kernel,
        grid=(m // bm,),
        in_specs=[pl.BlockSpec((bm, D_MODEL), lambda i: (i, 0)),
                  pl.BlockSpec((1, D_MODEL), lambda i: (0, 0)),
                  pl.BlockSpec((None, 1, D_MODEL), lambda i: (i // grp, 0, 0)),
                  pl.BlockSpec((None, 1, D_MODEL), lambda i: (i // grp, 0, 0))],
        out_specs=pl.BlockSpec((bm, D_MODEL), lambda i: (i, 0)),
        out_shape=jax.ShapeDtypeStruct((m, D_MODEL), BF16),
        compiler_params=_params(("arbitrary",)),
        name="normmod",
    )(x, g.reshape(1, D_MODEL), shift, scale)


def _rmsnorm_kernel(x_ref, g_ref, o_ref):
    x = x_ref[...]
    o_ref[...] = x * lax.rsqrt(jnp.mean(x * x, axis=-1, keepdims=True) + EPS) * g_ref[...]


def _final_norm(x, g, row0, rows):
    bm = 256
    off = row0 // bm
    return pl.pallas_call(
        _rmsnorm_kernel,
        grid=(rows // bm,),
        in_specs=[pl.BlockSpec((bm, D_MODEL), lambda i: (i + off, 0)),
                  pl.BlockSpec((1, D_MODEL), lambda i: (0, 0))],
        out_specs=pl.BlockSpec((bm, D_MODEL), lambda i: (i, 0)),
        out_shape=jax.ShapeDtypeStruct((rows, D_MODEL), F32),
        compiler_params=_params(("arbitrary",)),
        name="final_norm",
    )(x, g.reshape(1, D_MODEL))


def _swiglu_kernel(h_ref, wa_ref, wb_ref, o_ref, was, wbs):
    @pl.when(pl.program_id(1) == 0)
    def _():
        was[...] = wa_ref[...].astype(BF16)
        wbs[...] = wb_ref[...].astype(BF16)
    h = h_ref[...]
    a = _dot(h, was[...])
    b = _dot(h, wbs[...])
    o_ref[...] = (a * _sigmoid(a) * b).astype(o_ref.dtype)


def _mm_swiglu(h, w13):
    m, k = h.shape
    f = w13.shape[1] // 2
    bm, bn = 1024, 512
    nb = f // bn
    return pl.pallas_call(
        _swiglu_kernel,
        grid=(nb, m // bm),
        in_specs=[pl.BlockSpec((bm, k), lambda j, i: (i, 0)),
                  pl.BlockSpec((k, bn), lambda j, i: (0, j)),
                  pl.BlockSpec((k, bn), lambda j, i: (0, j + nb))],
        out_specs=pl.BlockSpec((bm, bn), lambda j, i: (i, j)),
        out_shape=jax.ShapeDtypeStruct((m, f), BF16),
        scratch_shapes=[pltpu.VMEM((k, bn), BF16), pltpu.VMEM((k, bn), BF16)],
        compiler_params=_params(("arbitrary", "arbitrary")),
        name="ffn_up",
    )(h, w13, w13)


def _mm_res_kernel(coef, a_ref, w_ref, x_ref, s_ref, o_ref, ws):
    @pl.when(pl.program_id(1) == 0)
    def _():
        ws[...] = w_ref[...].astype(BF16)
    acc = _dot(a_ref[...], ws[...])
    o_ref[...] = x_ref[...] + (coef * s_ref[...]) * acc


def _mm_res(a, w, x, s, coef):
    m, k = a.shape
    n = w.shape[1]
    bm, bn = 512, 512
    grp = ROW_GROUP // bm
    return pl.pallas_call(
        functools.partial(_mm_res_kernel, coef),
        grid=(n // bn, m // bm),
        in_specs=[pl.BlockSpec((bm, k), lambda j, i: (i, 0)),
                  pl.BlockSpec((k, bn), lambda j, i: (0, j)),
                  pl.BlockSpec((bm, bn), lambda j, i: (i, j)),
                  pl.BlockSpec((None, 1, bn), lambda j, i: (i // grp, 0, j))],
        out_specs=pl.BlockSpec((bm, bn), lambda j, i: (i, j)),
        out_shape=jax.ShapeDtypeStruct((m, n), F32),
        scratch_shapes=[pltpu.VMEM((k, bn), BF16)],
        compiler_params=_params(("arbitrary", "arbitrary")),
        name="mm_residual",
    )(a, w, x, s)


def _mm_bias_kernel(a_ref, w_ref, b_ref, o_ref):
    o_ref[...] = _dot(a_ref[...], w_ref[...]) + b_ref[...]


def _mm_bias(a, w, b, bn):
    m, k = a.shape
    n = w.shape[1]
    bm = 1024
    return pl.pallas_call(
        _mm_bias_kernel,
        grid=(n // bn, m // bm),
        in_specs=[pl.BlockSpec((bm, k), lambda j, i: (i, 0)),
                  pl.BlockSpec((k, bn), lambda j, i: (0, j)),
                  pl.BlockSpec((1, bn), lambda j, i: (0, j))],
        out_specs=pl.BlockSpec((bm, bn), lambda j, i: (i, j)),
        out_shape=jax.ShapeDtypeStruct((m, n), F32),
        compiler_params=_params(("arbitrary", "arbitrary")),
        name="in_proj",
    )(a, w, b.reshape(1, n))


def _merge_kernel(ya_ref, yb_ref, yc_ref, w_ref, ga_ref, gb_ref, gc_ref, o_ref, ws):
    @pl.when(pl.program_id(1) == 0)
    def _():
        ws[...] = w_ref[...].astype(BF16)
    acc = _sigmoid(ga_ref[...]).astype(F32) * _dot(ya_ref[...], ws[0])
    acc = acc + _sigmoid(gb_ref[...]) * _dot(yb_ref[...], ws[1])
    acc = acc + _sigmoid(gc_ref[...]) * _dot(yc_ref[...], ws[2])
    o_ref[...] = acc.astype(o_ref.dtype)


def _merge(ya, yb, yc, w_br, p):
    m = ya.shape[0]
    bm, bn = 512, 512
    g0 = 12 * W_MIX // bn
    gstep = D_MODEL // bn
    yspec = pl.BlockSpec((bm, W_MIX), lambda j, i: (i, 0))
    return pl.pallas_call(
        _merge_kernel,
        grid=(D_MODEL // bn, m // bm),
        in_specs=[yspec, yspec, yspec,
                  pl.BlockSpec((3, W_MIX, bn), lambda j, i: (0, 0, j)),
                  pl.BlockSpec((bm, bn), lambda j, i: (i, g0 + j)),
                  pl.BlockSpec((bm, bn), lambda j, i: (i, g0 + gstep + j)),
                  pl.BlockSpec((bm, bn), lambda j, i: (i, g0 + 2 * gstep + j))],
        out_specs=pl.BlockSpec((bm, bn), lambda j, i: (i, j)),
        out_shape=jax.ShapeDtypeStruct((m, D_MODEL), BF16),
        scratch_shapes=[pltpu.VMEM((3, W_MIX, bn), BF16)],
        compiler_params=_params(("arbitrary", "arbitrary")),
        name="branch_merge",
    )(ya, yb, yc, w_br, p, p, p)


def _gateprep_kernel(g_ref, pv_ref, gc_ref, grt_ref):
    bm = g_ref.shape[0]
    raw = g_ref[...]
    lane = lax.broadcasted_iota(jnp.int32, (bm, GATE_LANES), 1)
    row = lax.broadcasted_iota(jnp.int32, (bm, GATE_LANES), 0)
    fb = pv_ref[0:1, :]
    dtb = pv_ref[1:2, :]
    alog = pv_ref[2:3, :]
    lf = -_softplus(-(raw + fb))
    beta = _sigmoid(raw)
    gdec = -jnp.exp(alog) * _softplus(raw + dtb)
    act = jnp.where(lane < 16, raw, jnp.where(lane < 32, lf, jnp.where(lane < 48, beta, gdec)))
    seglen = jnp.where(lane < 32, CHUNK_A, CHUNK_B)
    rowmod = row & (seglen - 1)
    pre = act
    suf = act
    s = 1
    while s < max(CHUNK_A, CHUNK_B):
        pre = pre + jnp.where(rowmod >= s, pltpu.roll(pre, s, 0), 0.0)
        suf = suf + jnp.where(rowmod < seglen - s, pltpu.roll(suf, bm - s, 0), 0.0)
        s *= 2
    is_cum = ((lane >= 16) & (lane < 32)) | ((lane >= 48) & (lane < 64))
    backward = ((lane >> 3) & 1) == 1
    out = jnp.where(is_cum, jnp.where(backward, suf, pre), act)
    gc_ref[...] = out
    for t in range(bm // 128):
        grt_ref[:, t * 128:(t + 1) * 128] = out[t * 128:(t + 1) * 128, :].T


def _gateprep(graw, pv):
    m = graw.shape[0]
    bm = 512
    return pl.pallas_call(
        _gateprep_kernel,
        grid=(m // bm,),
        in_specs=[pl.BlockSpec((bm, GATE_LANES), lambda i: (i, 0)),
                  pl.BlockSpec((8, GATE_LANES), lambda i: (0, 0))],
        out_specs=[pl.BlockSpec((bm, GATE_LANES), lambda i: (i, 0)),
                   pl.BlockSpec((GATE_LANES, bm), lambda i: (0, i))],
        out_shape=[jax.ShapeDtypeStruct((m, GATE_LANES), F32),
                   jax.ShapeDtypeStruct((GATE_LANES, m), F32)],
        compiler_params=_params(("arbitrary",)),
        name="gate_prep",
    )(graw, pv)


def _gate_rows(grt, lane0, chunk):
    m = grt.shape[1]
    g = grt[lane0:lane0 + 32].reshape(4, N_HEADS, m // chunk, chunk)
    g = jnp.transpose(g, (1, 2, 0, 3))
    return jnp.pad(g, ((0, 0), (0, 0), (0, 4), (0, 0)))


def _select_cols(gc, lane0):
    h = pl.program_id(1)
    src = lax.broadcasted_iota(jnp.int32, (GATE_LANES, GATE_LANES), 0)
    c = lax.broadcasted_iota(jnp.int32, (GATE_LANES, GATE_LANES), 1)
    sel = ((src == lane0 + 8 * c + h) & (c < 4)).astype(BF16)
    return _dot_exact_rhs01(gc, sel)


def _head_out(hsum, g, gate):
    y = hsum * lax.rsqrt(jnp.mean(hsum * hsum, axis=-1, keepdims=True) + EPS) * g
    return y * gate


def _tri_masks(n):
    i = lax.broadcasted_iota(jnp.int32, (n, n), 0)
    j = lax.broadcasted_iota(jnp.int32, (n, n), 1)
    return i, j


def _mlstm_chunk(q, k, v, bcol, igcol, brow, igrow, blast, mask, c_ref, n_prev, m_prev):
    qb = q.astype(BF16)
    ks = k * HEAD_DIM ** -0.5
    vb = v.astype(BF16)
    logd = jnp.where(mask, bcol - brow + igrow, NEG_BIG)
    m_in = jnp.max(logd, axis=1, keepdims=True)
    m_i = jnp.maximum(bcol + m_prev, m_in)
    s = _dot_nt(qb, ks.astype(BF16)) * jnp.exp(logd - m_i)
    w_prev = jnp.exp(bcol + m_prev - m_i)
    c_prev = c_ref[...]
    num = w_prev * _dot(qb, c_prev.astype(BF16)) + _dot(s.astype(BF16), vb)
    den = w_prev * jnp.sum(q * n_prev, axis=1, keepdims=True) + jnp.sum(s, axis=1, keepdims=True)
    h = num / jnp.maximum(jnp.abs(den), jnp.exp(-m_i))
    m_end = jnp.max(blast - brow + igrow, axis=1, keepdims=True)
    m_new = jnp.maximum(blast + m_prev, m_end)
    w_s = jnp.exp(blast + m_prev - m_new)
    kw = ks * jnp.exp(blast - bcol + igcol - m_new)
    c_ref[...] = w_s * c_prev + _dot_tn(kw.astype(BF16), vb)
    n_new = w_s * n_prev + jnp.sum(kw, axis=0, keepdims=True)
    return h, n_new, m_new


def _mlstm_kernel(nseg, has_init, *refs):
    L = CHUNK_A
    if has_init:
        (q_ref, k_ref, v_ref, o_ref, gc_ref, rows_ref, hng_ref, c0_ref, n0_ref, m0_ref, _,
         y_ref, cout_ref, nout_ref, mout_ref, hf_ref, hb_ref, gh_ref, cf_ref, cb_ref) = refs
    else:
        (q_ref, k_ref, v_ref, o_ref, gc_ref, rows_ref, hng_ref,
         y_ref, cout_ref, nout_ref, mout_ref, hf_ref, hb_ref, gh_ref, cf_ref, cb_ref) = refs
    rb = q_ref.shape[0]
    segc = rb // nseg // L
    head = pl.program_id(1)
    gh_ref[...] = _select_cols(gc_ref[...], 0)
    i, j = _tri_masks(L)
    mask_f = i >= j
    mask_b = i <= j

    def segment(sg, carry):
        if has_init:
            cf_ref[...] = c0_ref[0]
            cb_ref[...] = c0_ref[1]
            st = (n0_ref[0, pl.ds(head, 1), :], m0_ref[0, pl.ds(head, 1), 0:1],
                  n0_ref[1, pl.ds(head, 1), :], m0_ref[1, pl.ds(head, 1), 0:1])
        else:
            cf_ref[...] = jnp.zeros((HEAD_DIM, HEAD_DIM), F32)
            cb_ref[...] = jnp.zeros((HEAD_DIM, HEAD_DIM), F32)
            zn = jnp.zeros((1, HEAD_DIM), F32)
            zm = jnp.zeros((1, 1), F32)
            st = (zn, zm, zn, zm)

        def step(c, st):
            nf, mf, nb, mb = st
            ch_f = sg * segc + c
            ch_b = sg * segc + (segc - 1 - c)
            rf = pl.multiple_of(ch_f * L, L)
            rbk = pl.multiple_of(ch_b * L, L)
            gf = gh_ref[pl.ds(rf, L), :]
            rw = rows_ref[ch_f]
            hf, nf, mf = _mlstm_chunk(
                q_ref[pl.ds(rf, L), :], k_ref[pl.ds(rf, L), :], v_ref[pl.ds(rf, L), :],
                gf[:, 2:3], gf[:, 0:1], rw[2:3, :], rw[0:1, :], rw[2:3, L - 1:L], mask_f,
                cf_ref, nf, mf)
            hf_ref[pl.ds(rf, L), :] = hf
            gb = gh_ref[pl.ds(rbk, L), :]
            rwb = rows_ref[ch_b]
            hb, nb, mb = _mlstm_chunk(
                q_ref[pl.ds(rbk, L), :], k_ref[pl.ds(rbk, L), :], v_ref[pl.ds(rbk, L), :],
                gb[:, 3:4], gb[:, 1:2], rwb[3:4, :], rwb[1:2, :], rwb[3:4, 0:1], mask_b,
                cb_ref, nb, mb)
            hb_ref[pl.ds(rbk, L), :] = hb
            return nf, mf, nb, mb

        nf, mf, nb, mb = lax.fori_loop(0, segc, step, st)
        cout_ref[sg, 0] = cf_ref[...]
        cout_ref[sg, 1] = cb_ref[...]
        nout_ref[sg, 0:1, :] = nf
        nout_ref[sg, 1:2, :] = nb
        mout_ref[sg, 0:1, :] = jnp.broadcast_to(mf, (1, HEAD_DIM))
        mout_ref[sg, 1:2, :] = jnp.broadcast_to(mb, (1, HEAD_DIM))
        return carry

    lax.fori_loop(0, nseg, segment, 0)
    y = _head_out(hf_ref[...] + hb_ref[...], hng_ref[...], _sigmoid(o_ref[...]))
    y_ref[...] = y.astype(y_ref.dtype)


def _ret_kernel(nseg, has_init, use_rope, *refs):
    L = CHUNK_C
    refs = list(refs)
    q_ref, k_ref, v_ref, g_ref, lg_ref, hng_ref = refs[:6]
    pos = 6
    if use_rope:
        cos_ref, sin_ref = refs[pos:pos + 2]
        pos += 2
    if has_init:
        s0_ref = refs[pos]
        pos += 2
    y_ref, sout_ref, of_ref, ob_ref, qs_ref, ks_ref, sf_ref, sb_ref = refs[pos:]
    rb = q_ref.shape[0]
    segc = rb // nseg // L

    q = q_ref[...] * HEAD_DIM ** -0.5
    k = k_ref[...]
    if use_rope:
        lane = lax.broadcasted_iota(jnp.int32, (rb, HEAD_DIM), 1)
        cos = cos_ref[...]
        sin = sin_ref[...]
        half = HEAD_DIM // 2

        def rope(x):
            rot = pltpu.roll(x, half, 1)
            return x * cos + jnp.where(lane < half, -rot, rot) * sin
        q = rope(q)
        k = rope(k)
    qs_ref[...] = q
    ks_ref[...] = k

    lg_f = lg_ref[0:1, 0:1]
    lg_b = lg_ref[1:2, 0:1]
    i, j = _tri_masks(L)
    dij = (i - j).astype(F32)
    decay_f = jnp.where(i >= j, jnp.exp(lg_f * jnp.maximum(dij, 0.0)), 0.0)
    decay_b = jnp.where(i <= j, jnp.exp(lg_b * jnp.maximum(-dij, 0.0)), 0.0)
    pos_c = lax.broadcasted_iota(jnp.int32, (L, 1), 0).astype(F32)
    qdec_f = jnp.exp(lg_f * (pos_c + 1.0))
    kend_f = jnp.exp(lg_f * (L - 1.0 - pos_c))
    qdec_b = jnp.exp(lg_b * (L - pos_c))
    kend_b = jnp.exp(lg_b * pos_c)
    cdec_f = jnp.exp(lg_f * L)
    cdec_b = jnp.exp(lg_b * L)

    def one(r, decay, qdec, kend, cdec, s_ref, o_ref):
        qc = qs_ref[pl.ds(r, L), :]
        kc = ks_ref[pl.ds(r, L), :]
        vb = v_ref[pl.ds(r, L), :].astype(BF16)
        inner = _dot((_dot_nt(qc.astype(BF16), kc.astype(BF16)) * decay).astype(BF16), vb)
        s_prev = s_ref[...]
        o_ref[pl.ds(r, L), :] = inner + _dot((qc * qdec).astype(BF16), s_prev.astype(BF16))
        s_ref[...] = cdec * s_prev + _dot_tn((kc * kend).astype(BF16), vb)

    def segment(sg, carry):
        if has_init:
            sf_ref[...] = s0_ref[0]
            sb_ref[...] = s0_ref[1]
        else:
            sf_ref[...] = jnp.zeros((HEAD_DIM, HEAD_DIM), F32)
            sb_ref[...] = jnp.zeros((HEAD_DIM, HEAD_DIM), F32)

        def step(c, carry2):
            rf = pl.multiple_of((sg * segc + c) * L, L)
            rbk = pl.multiple_of((sg * segc + (segc - 1 - c)) * L, L)
            one(rf, decay_f, qdec_f, kend_f, cdec_f, sf_ref, of_ref)
            one(rbk, decay_b, qdec_b, kend_b, cdec_b, sb_ref, ob_ref)
            return carry2

        lax.fori_loop(0, segc, step, 0)
        sout_ref[sg, 0] = sf_ref[...]
        sout_ref[sg, 1] = sb_ref[...]
        return carry

    lax.fori_loop(0, nseg, segment, 0)
    gate = g_ref[...]
    y = _head_out(of_ref[...] + ob_ref[...], hng_ref[...], gate * _sigmoid(gate))
    y_ref[...] = y.astype(y_ref.dtype)


def _retention(p, lgrows, hng, rope, y_prev, s0, *, blk0, nblk, nseg, layer, col0, rb=ROW_GROUP):
    m = p.shape[0]
    colspec = lambda c0: pl.BlockSpec((rb, HEAD_DIM), lambda b, h: (blk0 + b, c0 + h))
    in_specs = [colspec(col0), colspec(col0 + 8), colspec(col0 + 16), colspec(col0 + 24),
                pl.BlockSpec((None, 8, HEAD_DIM), lambda b, h: (h, 0, 0)),
                pl.BlockSpec((1, HEAD_DIM), lambda b, h: (0, 0))]
    args = [p, p, p, p, lgrows, hng]
    if rope is not None:
        in_specs += [pl.BlockSpec((rb, HEAD_DIM), lambda b, h: (0, 0))] * 2
        args += list(rope)
    aliases = {}
    if s0 is not None:
        in_specs += [pl.BlockSpec((None, None, 2, None, HEAD_DIM, HEAD_DIM), lambda b, h: (b, layer, 0, h, 0, 0)),
                     pl.BlockSpec(memory_space=pl.ANY)]
        args += [s0, y_prev]
        aliases = {len(args) - 1: 0}
    nsq = nblk * nseg
    return pl.pallas_call(
        functools.partial(_ret_kernel, nseg, s0 is not None, rope is not None),
        grid=(nblk, N_HEADS),
        in_specs=in_specs,
        out_specs=[pl.BlockSpec((rb, HEAD_DIM), lambda b, h: (blk0 + b, h)),
                   pl.BlockSpec((nseg, 2, None, HEAD_DIM, HEAD_DIM), lambda b, h: (b, 0, h, 0, 0))],
        out_shape=[jax.ShapeDtypeStruct((m, W_MIX), BF16),
                   jax.ShapeDtypeStruct((nsq, 2, N_HEADS, HEAD_DIM, HEAD_DIM), F32)],
        scratch_shapes=[pltpu.VMEM((rb, HEAD_DIM), F32), pltpu.VMEM((rb, HEAD_DIM), F32),
                        pltpu.VMEM((rb, HEAD_DIM), F32), pltpu.VMEM((rb, HEAD_DIM), F32),
                        pltpu.VMEM((HEAD_DIM, HEAD_DIM), F32), pltpu.VMEM((HEAD_DIM, HEAD_DIM), F32)],
        input_output_aliases=aliases,
        compiler_params=_params(("arbitrary", "arbitrary")),
        name="retention",
    )(*args)


def _inv_unit_triangular(a, eye):
    n = a.shape[0]
    b = -a
    x = eye + b
    bp = b
    p = 2
    while p < n:
        bpb = bp.astype(BF16)
        bp = _dot(bpb, bpb)
        x = x + _dot(x.astype(BF16), bp.astype(BF16))
        p *= 2
    r = eye - x - _dot_x3(a, x)
    return x + _dot(x.astype(BF16), r.astype(BF16))


def _delta_kernel(nseg, has_init, *refs):
    L = CHUNK_B
    refs = list(refs)
    q_ref, k_ref, v_ref, z_ref, cwq_ref, cwk_ref, cwv_ref, gc_ref, rows_ref, hng_ref = refs[:10]
    pos = 10
    if has_init:
        s0_ref = refs[pos]
        pos += 2
    (y_ref, sout_ref, qs_ref, ks_ref, vs_ref, gh_ref, u0f_ref, u0b_ref, wkf_ref, wkb_ref,
     qkf_ref, qkb_ref, of_ref, ob_ref, sf_ref, sb_ref) = refs[pos:]
    rb = q_ref.shape[0]
    tseg = rb // nseg
    segc = tseg // L
    nchunks = rb // L

    row = lax.broadcasted_iota(jnp.int32, (rb, HEAD_DIM), 0)
    rowmod = row % tseg
    first = rowmod == 0
    last = rowmod == tseg - 1

    def conv_silu(x_ref, w_ref):
        x = x_ref[...]
        prev = jnp.where(first, 0.0, pltpu.roll(x, 1, 0))
        nxt = jnp.where(last, 0.0, pltpu.roll(x, rb - 1, 0))
        y = w_ref[0:1, :] * prev + w_ref[1:2, :] * x + w_ref[2:3, :] * nxt
        return y * _sigmoid(y)

    def l2n(x):
        return x * lax.rsqrt(jnp.sum(x * x, axis=-1, keepdims=True) + EPS)

    qs_ref[...] = l2n(conv_silu(q_ref, cwq_ref)) * HEAD_DIM ** -0.5
    ks_ref[...] = l2n(conv_silu(k_ref, cwk_ref))
    vs_ref[...] = conv_silu(v_ref, cwv_ref)
    gh_ref[...] = _select_cols(gc_ref[...], 32)

    i, j = _tri_masks(L)
    eye = (i == j).astype(F32)
    masks = ((i >= j, i > j), (i <= j, i < j))

    def prep(c, carry):
        r = pl.multiple_of(c * L, L)
        qc = qs_ref[pl.ds(r, L), :]
        kc = ks_ref[pl.ds(r, L), :]
        vc = vs_ref[pl.ds(r, L), :]
        gcol = gh_ref[pl.ds(r, L), :]
        rw = rows_ref[c]
        kb = kc.astype(BF16)
        kk = _dot_nt(kb, kb)
        qk = _dot_nt(qc.astype(BF16), kb)
        for d, (u0_ref, wk_ref, qkd_ref) in enumerate(((u0f_ref, wkf_ref, qkf_ref), (u0b_ref, wkb_ref, qkb_ref))):
            incl, strict = masks[d]
            beta_c = gcol[:, d:d + 1]
            g_c = gcol[:, 2 + d:3 + d]
            g_r = rw[2 + d:3 + d, :]
            decay = jnp.where(incl, jnp.exp(jnp.where(incl, g_c - g_r, 0.0)), 0.0)
            a = jnp.where(strict, kk * decay, 0.0) * beta_c
            t = _inv_unit_triangular(a, eye)
            rhs = jnp.concatenate([beta_c * vc, (beta_c * jnp.exp(g_c)) * kc], axis=1)
            sol = _dot(t.astype(BF16), rhs.astype(BF16))
            u0_ref[pl.ds(r, L), :] = sol[:, :HEAD_DIM]
            wk_ref[pl.ds(r, L), :] = sol[:, HEAD_DIM:].astype(wk_ref.dtype)
            qkd_ref[pl.ds(r, L), :] = (qk * decay).astype(qkd_ref.dtype)
        return carry

    lax.fori_loop(0, nchunks, prep, 0)

    def one(r, d, glast, u0_ref, wk_ref, qkd_ref, s_ref, o_ref):
        g_c = gh_ref[pl.ds(r, L), :][:, 2 + d:3 + d]
        s_prev = s_ref[...]
        sb = s_prev.astype(BF16)
        u = u0_ref[pl.ds(r, L), :] - _dot(wk_ref[pl.ds(r, L), :], sb)
        ub = u.astype(BF16)
        qd = (qs_ref[pl.ds(r, L), :] * jnp.exp(g_c)).astype(BF16)
        o_ref[pl.ds(r, L), :] = _dot(qd, sb) + _dot(qkd_ref[pl.ds(r, L), :], ub)
        ke = (ks_ref[pl.ds(r, L), :] * jnp.exp(glast - g_c)).astype(BF16)
        s_ref[...] = jnp.exp(glast) * s_prev + _dot_tn(ke, ub)

    def segment(sg, carry):
        if has_init:
            sf_ref[...] = s0_ref[0]
            sb_ref[...] = s0_ref[1]
        else:
            sf_ref[...] = jnp.zeros((HEAD_DIM, HEAD_DIM), F32)
            sb_ref[...] = jnp.zeros((HEAD_DIM, HEAD_DIM), F32)

        def step(c, carry2):
            ch_f = sg * segc + c
            ch_b = sg * segc + (segc - 1 - c)
            one(pl.multiple_of(ch_f * L, L), 0, rows_ref[ch_f][2:3, L - 1:L],
                u0f_ref, wkf_ref, qkf_ref, sf_ref, of_ref)
            one(pl.multiple_of(ch_b * L, L), 1, rows_ref[ch_b][3:4, 0:1],
                u0b_ref, wkb_ref, qkb_ref, sb_ref, ob_ref)
            return carry2

        lax.fori_loop(0, segc, step, 0)
        sout_ref[sg, 0] = sf_ref[...]
        sout_ref[sg, 1] = sb_ref[...]
        return carry

    lax.fori_loop(0, nseg, segment, 0)
    gate = z_ref[...]
    y = _head_out(of_ref[...] + ob_ref[...], hng_ref[...], gate * _sigmoid(gate))
    y_ref[...] = y.astype(y_ref.dtype)


def _delta(p, conv_w, gc, rows, hng, y_prev, s0, *, blk0, nblk, nseg, layer, col0, rb=ROW_GROUP):
    m = p.shape[0]
    cb = rb // CHUNK_B
    colspec = lambda c0: pl.BlockSpec((rb, HEAD_DIM), lambda b, h: (blk0 + b, c0 + h))
    cwspec = lambda c0: pl.BlockSpec((CONV_K, HEAD_DIM), lambda b, h: (0, c0 + h))
    in_specs = [colspec(col0), colspec(col0 + 8), colspec(col0 + 16), colspec(col0 + 24),
                cwspec(0), cwspec(8), cwspec(16),
                pl.BlockSpec((rb, GATE_LANES), lambda b, h: (blk0 + b, 0)),
                pl.BlockSpec((None, cb, 8, CHUNK_B), lambda b, h: (h, blk0 + b, 0, 0)),
                pl.BlockSpec((1, HEAD_DIM), lambda b, h: (0, 0))]
    args = [p, p, p, p, conv_w, conv_w, conv_w, gc, rows, hng]
    aliases = {}
    if s0 is not None:
        in_specs += [pl.BlockSpec((None, None, 2, None, HEAD_DIM, HEAD_DIM), lambda b, h: (b, layer, 0, h, 0, 0)),
                     pl.BlockSpec(memory_space=pl.ANY)]
        args += [s0, y_prev]
        aliases = {len(args) - 1: 0}
    nsq = nblk * nseg
    big = lambda dt: pltpu.VMEM((rb, HEAD_DIM), dt)
    return pl.pallas_call(
        functools.partial(_delta_kernel, nseg, s0 is not None),
        grid=(nblk, N_HEADS),
        in_specs=in_specs,
        out_specs=[pl.BlockSpec((rb, HEAD_DIM), lambda b, h: (blk0 + b, h)),
                   pl.BlockSpec((nseg, 2, None, HEAD_DIM, HEAD_DIM), lambda b, h: (b, 0, h, 0, 0))],
        out_shape=[jax.ShapeDtypeStruct((m, W_MIX), BF16),
                   jax.ShapeDtypeStruct((nsq, 2, N_HEADS, HEAD_DIM, HEAD_DIM), F32)],
        scratch_shapes=[big(F32), big(F32), big(F32), pltpu.VMEM((rb, GATE_LANES), F32),
                        big(F32), big(F32), big(BF16), big(BF16),
                        pltpu.VMEM((rb, CHUNK_B), BF16), pltpu.VMEM((rb, CHUNK_B), BF16),
                        big(F32), big(F32),
                        pltpu.VMEM((HEAD_DIM, HEAD_DIM), F32), pltpu.VMEM((HEAD_DIM, HEAD_DIM), F32)],
        input_output_aliases=aliases,
        compiler_params=_params(("arbitrary", "arbitrary")),
        name="deltanet",
    )(*args)


def _state_specs(nseg):
    specs = [pl.BlockSpec((nseg, 2, None, HEAD_DIM, HEAD_DIM), lambda b, h: (b, 0, h, 0, 0)),
             pl.BlockSpec((None, nseg, 2, HEAD_DIM), lambda b, h: (h, b, 0, 0)),
             pl.BlockSpec((None, nseg, 2, HEAD_DIM), lambda b, h: (h, b, 0, 0))]
    return specs


def _mlstm(p, gc, rows, hng, y_prev, init, *, blk0, nblk, nseg, layer, rb=ROW_GROUP):
    m = p.shape[0]
    cb = rb // CHUNK_A
    colspec = lambda c0: pl.BlockSpec((rb, HEAD_DIM), lambda b, h: (blk0 + b, c0 + h))
    in_specs = [colspec(0), colspec(8), colspec(16), colspec(24),
                pl.BlockSpec((rb, GATE_LANES), lambda b, h: (blk0 + b, 0)),
                pl.BlockSpec((None, cb, 8, CHUNK_A), lambda b, h: (h, blk0 + b, 0, 0)),
                pl.BlockSpec((1, HEAD_DIM), lambda b, h: (0, 0))]
    args = [p, p, p, p, gc, rows, hng]
    if init is not None:
        c0, n0, m0 = init
        in_specs += [pl.BlockSpec((None, None, 2, None, HEAD_DIM, HEAD_DIM), lambda b, h: (b, layer, 0, h, 0, 0)),
                     pl.BlockSpec((None, None, 2, N_HEADS, HEAD_DIM), lambda b, h: (b, layer, 0, 0, 0)),
                     pl.BlockSpec((None, None, 2, N_HEADS, HEAD_DIM), lambda b, h: (b, layer, 0, 0, 0)),
                     pl.BlockSpec(memory_space=pl.ANY)]
        args += [c0, n0, m0, y_prev]
        aliases = {len(args) - 1: 0}
    else:
        aliases = {}
    nsq = nblk * nseg
    out_shape = [jax.ShapeDtypeStruct((m, W_MIX), BF16),
                 jax.ShapeDtypeStruct((nsq, 2, N_HEADS, HEAD_DIM, HEAD_DIM), F32),
                 jax.ShapeDtypeStruct((N_HEADS, nsq, 2, HEAD_DIM), F32),
                 jax.ShapeDtypeStruct((N_HEADS, nsq, 2, HEAD_DIM), F32)]
    out_specs = [pl.BlockSpec((rb, HEAD_DIM), lambda b, h: (blk0 + b, h))] + _state_specs(nseg)
    return pl.pallas_call(
        functools.partial(_mlstm_kernel, nseg, init is not None),
        grid=(nblk, N_HEADS),
        in_specs=in_specs, out_specs=out_specs, out_shape=out_shape,
        scratch_shapes=[pltpu.VMEM((rb, HEAD_DIM), F32), pltpu.VMEM((rb, HEAD_DIM), F32),
                        pltpu.VMEM((rb, GATE_LANES), F32),
                        pltpu.VMEM((HEAD_DIM, HEAD_DIM), F32), pltpu.VMEM((HEAD_DIM, HEAD_DIM), F32)],
        input_output_aliases=aliases,
        compiler_params=_params(("arbitrary", "arbitrary")),
        name="mlstm",
    )(*args)


_MAIN_COLS = ((0, 4 * W_MIX), (4 * W_MIX + 32, 8 * W_MIX + 32), (8 * W_MIX + 64, 12 * W_MIX + 64 + 3 * D_MODEL))
_GATE_COLS = ((4 * W_MIX, 4 * W_MIX + 32), (8 * W_MIX + 32, 8 * W_MIX + 64))
COL_A, COL_B, COL_C = 0, 32, 64


def _take_cols(w, ranges):
    return jnp.concatenate([w[..., a:b] for a, b in ranges], axis=-1)


def _rope_tables(t):
    pos = jnp.arange(t)
    pos_r = (pos // GRID_W).astype(F32)
    pos_c = (pos % GRID_W).astype(F32)
    nf = HEAD_DIM // 4
    freqs = ROPE_BASE ** (-jnp.arange(nf, dtype=F32) / nf)
    ang = jnp.concatenate([pos_r[:, None] * freqs, pos_c[:, None] * freqs], axis=-1)
    ang = jnp.concatenate([ang, ang], axis=-1)
    return jnp.cos(ang), jnp.sin(ang)


def kernel(x_prompt, x_sample, state_mlstm_C, state_mlstm_n, state_mlstm_m, state_delta_S, state_ret_S,
           c, c_ctx, norm_g, final_norm_g, w_ada, b_ada, w_in, b_in, mlstm_f_bias, conv_w, delta_A_log,
           delta_dt_bias, ret_log_gamma, head_norm_g, w_br, w_out, ffn_w13, ffn_w2):
    bp, sp, d = x_prompt.shape
    bs, ss, _ = x_sample.shape
    assert d == D_MODEL and bp * sp == ROW_GROUP and ss == ROW_GROUP
    n_prompt_blk, n_sample_blk = 1, bs
    x = jnp.concatenate([x_prompt.reshape(-1, d), x_sample.reshape(-1, d)], axis=0)

    cv8 = jnp.zeros((8, d), F32).at[0].set(c_ctx).at[1:1 + bs].set(c)
    mod = _ada(cv8, w_ada, b_ada)[:, :1 + bs].reshape(DEPTH, 1 + bs, N_MOD, 1, d)
    rope = _rope_tables(ss)
    m0_pad = jnp.broadcast_to(state_mlstm_m[..., None], state_mlstm_m.shape + (HEAD_DIM,))

    w_main = _take_cols(w_in, _MAIN_COLS).astype(BF16)
    b_main = _take_cols(b_in, _MAIN_COLS)
    pad = GATE_LANES - 64
    w_gate = jnp.pad(_take_cols(w_in, _GATE_COLS), ((0, 0), (0, 0), (0, pad))).astype(BF16)
    b_gate = jnp.pad(_take_cols(b_in, _GATE_COLS), ((0, 0), (0, pad)))

    new_c, new_n, new_m, new_sd, new_sr = [], [], [], [], []
    for l in range(DEPTH):
        md = lambda i: mod[l, :, i]
        h = _normmod(x, norm_g[l, 0], md(0), md(1))
        x = _mm_res(_mm_swiglu(h, ffn_w13[l, 0]), ffn_w2[l, 0], x, md(2), 0.5)

        h = _normmod(x, norm_g[l, 1], md(3), md(4))
        p = _mm_bias(h, w_main[l], b_main[l], 1024)
        graw = _mm_bias(h, w_gate[l], b_gate[l], GATE_LANES)
        pv = jnp.zeros((8, GATE_LANES), F32)
        pv = pv.at[0, 16:32].set(mlstm_f_bias[l].reshape(-1))
        pv = pv.at[1, 48:64].set(delta_dt_bias[l].reshape(-1))
        pv = pv.at[2, 48:64].set(delta_A_log[l].reshape(-1))
        gc, grt = _gateprep(graw, pv)
        rows_a = _gate_rows(grt, 0, CHUNK_A)
        rows_b = _gate_rows(grt, 32, CHUNK_B)
        hng = head_norm_g[l].reshape(3, 1, HEAD_DIM)
        lgrows = jnp.zeros((N_HEADS, 8, HEAD_DIM), F32).at[:, 0:2, :].set(
            jnp.broadcast_to(ret_log_gamma[l].T[:, :, None], (N_HEADS, 2, HEAD_DIM)))
        prompt = dict(blk0=0, nblk=n_prompt_blk, nseg=bp, layer=l)
        sample = dict(blk0=n_prompt_blk, nblk=n_sample_blk, nseg=1, layer=l)

        ya, c_new, n_new, m_new = _mlstm(p, gc, rows_a, hng[0], None, None, **prompt)
        ya = _mlstm(p, gc, rows_a, hng[0], ya, (state_mlstm_C, state_mlstm_n, m0_pad), **sample)[0]
        yb, sd_new = _delta(p, conv_w[l], gc, rows_b, hng[1], None, None, col0=COL_B, **prompt)
        yb = _delta(p, conv_w[l], gc, rows_b, hng[1], yb, state_delta_S, col0=COL_B, **sample)[0]
        yc, sr_new = _retention(p, lgrows, hng[2], None, None, None, col0=COL_C, **prompt)
        yc = _retention(p, lgrows, hng[2], rope, yc, state_ret_S, col0=COL_C, **sample)[0]
        new_c.append(c_new)
        new_n.append(jnp.transpose(n_new, (1, 2, 0, 3)))
        new_m.append(jnp.transpose(m_new[..., 0], (1, 2, 0)))
        new_sd.append(sd_new)
        new_sr.append(sr_new)

        x = _mm_res(_merge(ya, yb, yc, w_br[l], p), w_out[l], x, md(5), 1.0)
        h = _normmod(x, norm_g[l, 2], md(6), md(7))
        x = _mm_res(_mm_swiglu(h, ffn_w13[l, 1]), ffn_w2[l, 1], x, md(8), 0.5)

    y_prompt = _final_norm(x, final_norm_g, 0, bp * sp).reshape(bp, sp, d)
    y_sample = _final_norm(x, final_norm_g, bp * sp, bs * ss).reshape(bs, ss, d)
    stack = lambda xs: jnp.stack(xs, axis=1)
    return (y_prompt, y_sample, stack(new_c), stack(new_n), stack(new_m), stack(new_sd), stack(new_sr))
```

```python
import functools

import jax
import jax.numpy as jnp
import numpy as np
from jax import lax
from jax.experimental import pallas as pl
from jax.experimental.pallas import tpu as pltpu

F32 = jnp.float32
BF16 = jnp.bfloat16

D_MODEL = 2048
DEPTH = 2
N_HEADS = 8
HEAD_DIM = 128
W_MIX = N_HEADS * HEAD_DIM
D_FF = 4096
N_MOD = 9
CONV_K = 3
GRID_W = 64
ROPE_BASE = 10000.0
EPS = 1e-6
ROW_GROUP = 4096
N_MAIN = 12 * W_MIX + 3 * D_MODEL
GATE_LANES = 128
CHUNK_A = 128
CHUNK_B = 128
RET_GROUP = 4
MLSTM_GROUP = 2
PREP_GROUP = 8
CHUNK_C = 128
NEG_BIG = -1e30
VMEM_LIMIT = 56 * 1024 * 1024


def _params(sem):
    return pltpu.CompilerParams(dimension_semantics=sem, vmem_limit_bytes=VMEM_LIMIT)


def _dot(a, b):
    return jnp.dot(a, b, preferred_element_type=F32)


def _dot_nt(a, b):
    return lax.dot_general(a, b, (((1,), (1,)), ((), ())), preferred_element_type=F32)


def _dot_tn(a, b):
    return lax.dot_general(a, b, (((0,), (0,)), ((), ())), preferred_element_type=F32)


def _split3(x):
    hi = x.astype(BF16)
    r1 = x - hi.astype(F32)
    mid = r1.astype(BF16)
    lo = (r1 - mid.astype(F32)).astype(BF16)
    return hi, mid, lo


def _dot_exact_rhs01(x, sel):
    hi, mid, lo = _split3(x)
    return _dot(hi, sel) + _dot(mid, sel) + _dot(lo, sel)


def _dot_x3(a, b):
    ah = a.astype(BF16)
    al = (a - ah.astype(F32)).astype(BF16)
    bh = b.astype(BF16)
    bl = (b - bh.astype(F32)).astype(BF16)
    return _dot(ah, bh) + _dot(ah, bl) + _dot(al, bh)


def _sigmoid(x):
    return 1.0 / (1.0 + jnp.exp(-x))


def _softplus(x):
    return jnp.maximum(x, 0.0) + jnp.log(1.0 + jnp.exp(-jnp.abs(x)))


def _ada_kernel(cv_ref, w_ref, b_ref, o_ref):
    cv = cv_ref[...]
    s = (cv * _sigmoid(cv)).astype(BF16)
    o_ref[...] = _dot(s, w_ref[...].astype(BF16)) + b_ref[...]


def _ada(cv8, w_ada, b_ada):
    n = w_ada.shape[-1]
    bn = 1024
    return pl.pallas_call(
        _ada_kernel,
        grid=(DEPTH, n // bn),
        in_specs=[pl.BlockSpec((8, D_MODEL), lambda l, j: (0, 0)),
                  pl.BlockSpec((None, D_MODEL, bn), lambda l, j: (l, 0, j)),
                  pl.BlockSpec((None, 1, bn), lambda l, j: (l, 0, j))],
        out_specs=pl.BlockSpec((None, 8, bn), lambda l, j: (l, 0, j)),
        out_shape=jax.ShapeDtypeStruct((DEPTH, 8, n), F32),
        compiler_params=_params(("arbitrary", "arbitrary")),
        name="adaln",
    )(cv8, w_ada, b_ada.reshape(DEPTH, 1, n))


def _normmod_kernel(x_ref, g_ref, sh_ref, sc_ref, o_ref):
    x = x_ref[...]
    y = x * lax.rsqrt(jnp.mean(x * x, axis=-1, keepdims=True) + EPS) * g_ref[...]
    o_ref[...] = (y * (1.0 + sc_ref[...]) + sh_ref[...]).astype(o_ref.dtype)


def _normmod(x, g, shift, scale):
    m = x.shape[0]
    bm = 256
    grp = ROW_GROUP // bm
    return pl.pallas_call(
        _normmod_kernel,
        grid=(m // bm,),
        in_specs=[pl.BlockSpec((bm, D_MODEL), lambda i: (i, 0)),
                  pl.BlockSpec((1, D_MODEL), lambda i: (0, 0)),
                  pl.BlockSpec((None, 1, D_MODEL), lambda i: (i // grp, 0, 0)),
                  pl.BlockSpec((None, 1, D_MODEL), lambda i: (i // grp, 0, 0))],
        out_specs=pl.BlockSpec((bm, D_MODEL), lambda i: (i, 0)),
        out_shape=jax.ShapeDtypeStruct((m, D_MODEL), BF16),
        compiler_params=_params(("arbitrary",)),
        name="normmod",
    )(x, g.reshape(1, D_MODEL), shift, scale)


def _rmsnorm_kernel(x_ref, g_ref, o_ref):
    x = x_ref[...]
    o_ref[...] = x * lax.rsqrt(jnp.mean(x * x, axis=-1, keepdims=True) + EPS) * g_ref[...]


def _final_norm(x, g, row0, rows):
    bm = 256
    off = row0 // bm
    return pl.pallas_call(
        _rmsnorm_kernel,
        grid=(rows // bm,),
        in_specs=[pl.BlockSpec((bm, D_MODEL), lambda i: (i + off, 0)),
                  pl.BlockSpec((1, D_MODEL), lambda i: (0, 0))],
        out_specs=pl.BlockSpec((bm, D_MODEL), lambda i: (i, 0)),
        out_shape=jax.ShapeDtypeStruct((rows, D_MODEL), F32),
        compiler_params=_params(("arbitrary",)),
        name="final_norm",
    )(x, g.reshape(1, D_MODEL))


def _swiglu_kernel(h_ref, wa_ref, wb_ref, o_ref, was, wbs):
    @pl.when(pl.program_id(1) == 0)
    def _():
        was[...] = wa_ref[...].astype(BF16)
        wbs[...] = wb_ref[...].astype(BF16)
    h = h_ref[...]
    a = _dot(h, was[...])
    b = _dot(h, wbs[...])
    o_ref[...] = (a * _sigmoid(a) * b).astype(o_ref.dtype)


def _wspec(widx, k, bn, col=lambda j: j):
    lead = tuple(widx)
    return pl.BlockSpec((None,) * len(lead) + (k, bn), lambda j, i: lead + (0, col(j)))


def _mm_swiglu(h, w13, widx):
    m, k = h.shape
    f = w13.shape[-1] // 2
    bm, bn = 1024, 512
    nb = f // bn
    return pl.pallas_call(
        _swiglu_kernel,
        grid=(nb, m // bm),
        in_specs=[pl.BlockSpec((bm, k), lambda j, i: (i, 0)),
                  _wspec(widx, k, bn),
                  _wspec(widx, k, bn, lambda j: j + nb)],
        out_specs=pl.BlockSpec((bm, bn), lambda j, i: (i, j)),
        out_shape=jax.ShapeDtypeStruct((m, f), BF16),
        scratch_shapes=[pltpu.VMEM((k, bn), BF16), pltpu.VMEM((k, bn), BF16)],
        compiler_params=_params(("arbitrary", "arbitrary")),
        name="ffn_up",
    )(h, w13, w13)


def _mm_res_kernel(coef, a_ref, w_ref, x_ref, s_ref, o_ref, ws):
    @pl.when(pl.program_id(1) == 0)
    def _():
        ws[...] = w_ref[...].astype(BF16)
    acc = _dot(a_ref[...], ws[...])
    o_ref[...] = x_ref[...] + (coef * s_ref[...]) * acc


def _mm_res(a, w, widx, x, s, coef, bm, bn):
    m, k = a.shape
    n = w.shape[-1]
    grp = ROW_GROUP // bm
    return pl.pallas_call(
        functools.partial(_mm_res_kernel, coef),
        grid=(n // bn, m // bm),
        in_specs=[pl.BlockSpec((bm, k), lambda j, i: (i, 0)),
                  _wspec(widx, k, bn),
                  pl.BlockSpec((bm, bn), lambda j, i: (i, j)),
                  pl.BlockSpec((None, 1, bn), lambda j, i: (i // grp, 0, j))],
        out_specs=pl.BlockSpec((bm, bn), lambda j, i: (i, j)),
        out_shape=jax.ShapeDtypeStruct((m, n), F32),
        scratch_shapes=[pltpu.VMEM((k, bn), BF16)],
        compiler_params=_params(("arbitrary", "arbitrary")),
        name="mm_residual",
    )(a, w, x, s)


def _mm_bias_kernel(a_ref, w_ref, b_ref, o_ref):
    o_ref[...] = _dot(a_ref[...], w_ref[...]) + b_ref[...]


def _mm_bias(a, w, b, layer, bn):
    m, k = a.shape
    n = w.shape[-1]
    bm = 1024
    return pl.pallas_call(
        _mm_bias_kernel,
        grid=(n // bn, m // bm),
        in_specs=[pl.BlockSpec((bm, k), lambda j, i: (i, 0)),
                  _wspec((layer,), k, bn),
                  pl.BlockSpec((None, 1, bn), lambda j, i: (layer, 0, j))],
        out_specs=pl.BlockSpec((bm, bn), lambda j, i: (i, j)),
        out_shape=jax.ShapeDtypeStruct((m, n), F32),
        compiler_params=_params(("arbitrary", "arbitrary")),
        name="in_proj",
    )(a, w, b.reshape(DEPTH, 1, n))


def _merge_kernel(ya_ref, yb_ref, yc_ref, w_ref, ga_ref, gb_ref, gc_ref, o_ref, ws):
    @pl.when(pl.program_id(1) == 0)
    def _():
        ws[...] = w_ref[...].astype(BF16)
    acc = _sigmoid(ga_ref[...]).astype(F32) * _dot(ya_ref[...], ws[0])
    acc = acc + _sigmoid(gb_ref[...]) * _dot(yb_ref[...], ws[1])
    acc = acc + _sigmoid(gc_ref[...]) * _dot(yc_ref[...], ws[2])
    o_ref[...] = acc.astype(o_ref.dtype)


def _merge(ya, yb, yc, w_br, layer, p):
    m = ya.shape[0]
    bm, bn = 512, 512
    g0 = 12 * W_MIX // bn
    gstep = D_MODEL // bn
    yspec = pl.BlockSpec((bm, W_MIX), lambda j, i: (i, 0))
    return pl.pallas_call(
        _merge_kernel,
        grid=(D_MODEL // bn, m // bm),
        in_specs=[yspec, yspec, yspec,
                  pl.BlockSpec((None, 3, W_MIX, bn), lambda j, i: (layer, 0, 0, j)),
                  pl.BlockSpec((bm, bn), lambda j, i: (i, g0 + j)),
                  pl.BlockSpec((bm, bn), lambda j, i: (i, g0 + gstep + j)),
                  pl.BlockSpec((bm, bn), lambda j, i: (i, g0 + 2 * gstep + j))],
        out_specs=pl.BlockSpec((bm, bn), lambda j, i: (i, j)),
        out_shape=jax.ShapeDtypeStruct((m, D_MODEL), BF16),
        scratch_shapes=[pltpu.VMEM((3, W_MIX, bn), BF16)],
        compiler_params=_params(("arbitrary", "arbitrary")),
        name="branch_merge",
    )(ya, yb, yc, w_br, p, p, p)


def _gateprep_kernel(g_ref, pv_ref, gc_ref, grt_ref):
    bm = g_ref.shape[0]
    raw = g_ref[...]
    lane = lax.broadcasted_iota(jnp.int32, (bm, GATE_LANES), 1)
    row = lax.broadcasted_iota(jnp.int32, (bm, GATE_LANES), 0)
    fb = pv_ref[0:1, :]
    dtb = pv_ref[1:2, :]
    alog = pv_ref[2:3, :]
    lf = -_softplus(-(raw + fb))
    beta = _sigmoid(raw)
    gdec = -jnp.exp(alog) * _softplus(raw + dtb)
    act = jnp.where(lane < 16, raw, jnp.where(lane < 32, lf, jnp.where(lane < 48, beta, gdec)))
    seglen = jnp.where(lane < 32, CHUNK_A, CHUNK_B)
    rowmod = row & (seglen - 1)
    pre = act
    suf = act
    s = 1
    while s < max(CHUNK_A, CHUNK_B):
        pre = pre + jnp.where(rowmod >= s, pltpu.roll(pre, s, 0), 0.0)
        suf = suf + jnp.where(rowmod < seglen - s, pltpu.roll(suf, bm - s, 0), 0.0)
        s *= 2
    is_cum = ((lane >= 16) & (lane < 32)) | ((lane >= 48) & (lane < 64))
    backward = ((lane >> 3) & 1) == 1
    out = jnp.where(is_cum, jnp.where(backward, suf, pre), act)
    gc_ref[...] = out
    for t in range(bm // 128):
        grt_ref[:, t * 128:(t + 1) * 128] = out[t * 128:(t + 1) * 128, :].T


def _gateprep(graw, pv):
    m = graw.shape[0]
    bm = 512
    return pl.pallas_call(
        _gateprep_kernel,
        grid=(m // bm,),
        in_specs=[pl.BlockSpec((bm, GATE_LANES), lambda i: (i, 0)),
                  pl.BlockSpec((8, GATE_LANES), lambda i: (0, 0))],
        out_specs=[pl.BlockSpec((bm, GATE_LANES), lambda i: (i, 0)),
                   pl.BlockSpec((GATE_LANES, bm), lambda i: (0, i))],
        out_shape=[jax.ShapeDtypeStruct((m, GATE_LANES), F32),
                   jax.ShapeDtypeStruct((GATE_LANES, m), F32)],
        compiler_params=_params(("arbitrary",)),
        name="gate_prep",
    )(graw, pv)


def _gate_rows(grt, lane0, chunk, chunk_major=True):
    m = grt.shape[1]
    g = grt[lane0:lane0 + 32].reshape(4, N_HEADS, m // chunk, chunk)
    if chunk_major:
        return jnp.pad(jnp.transpose(g, (1, 2, 0, 3)), ((0, 0), (0, 0), (0, 4), (0, 0)))
    return jnp.pad(jnp.transpose(g, (1, 0, 2, 3)), ((0, 0), (0, 4), (0, 0), (0, 0)))


def _select_cols(gc, lane0):
    h = pl.program_id(1)
    src = lax.broadcasted_iota(jnp.int32, (GATE_LANES, GATE_LANES), 0)
    c = lax.broadcasted_iota(jnp.int32, (GATE_LANES, GATE_LANES), 1)
    sel = ((src == lane0 + 8 * c + h) & (c < 4)).astype(BF16)
    return _dot_exact_rhs01(gc, sel)


def _head_out(hsum, g, gate):
    y = hsum * lax.rsqrt(jnp.mean(hsum * hsum, axis=-1, keepdims=True) + EPS) * g
    return y * gate


def _tri_masks(n):
    i = lax.broadcasted_iota(jnp.int32, (n, n), 0)
    j = lax.broadcasted_iota(jnp.int32, (n, n), 1)
    return i, j


def _mlstm_kernel(nseg, has_init, *refs):
    L = CHUNK_A
    refs = list(refs)
    q_ref, k_ref, v_ref, o_ref, gc_ref, rows_ref, hng_ref = refs[:7]
    pos = 7
    if has_init:
        c0_ref, n0_ref, m0_ref = refs[pos:pos + 3]
        pos += 4
    (y_ref, cout_ref, nout_ref, mout_ref,
     hf_ref, hb_ref, gh_ref, sc_ref, kv_ref, ksum_ref, cp_ref, np_ref) = refs[pos:]
    h_refs = (hf_ref, hb_ref)
    rb = q_ref.shape[0]
    nct = rb // L
    segc = nct // nseg
    head = pl.program_id(1)
    scale = HEAD_DIM ** -0.5
    gh_ref[...] = _select_cols(gc_ref[...], 0)
    i, j = _tri_masks(L)
    masks = (i >= j, i <= j)
    lanes = (nct, HEAD_DIM)
    for d in range(2):
        ig = rows_ref[d]
        b = rows_ref[2 + d]
        bl = b[:, L - 1:L] if d == 0 else b[:, 0:1]
        sc_ref[d, 0] = jnp.broadcast_to(bl, lanes)
        sc_ref[d, 1] = jnp.broadcast_to(jnp.max(bl - b + ig, axis=1, keepdims=True), lanes)

    def chunk_of(sg, c, d):
        return sg * segc + (c if d == 0 else segc - 1 - c)

    def scan_m(sg, carry):
        if has_init:
            ms = tuple(m0_ref[d, pl.ds(head, 1), :] for d in range(2))
        else:
            ms = (jnp.zeros((1, HEAD_DIM), F32),) * 2

        def step(c, ms):
            out = []
            for d in range(2):
                n = chunk_of(sg, c, d)
                sc_ref[d, 2, pl.ds(n, 1), :] = ms[d]
                m_new = jnp.maximum(sc_ref[d, 0, pl.ds(n, 1), :] + ms[d], sc_ref[d, 1, pl.ds(n, 1), :])
                sc_ref[d, 3, pl.ds(n, 1), :] = m_new
                out.append(m_new)
            return tuple(out)

        ms = lax.fori_loop(0, segc, step, ms)
        for d in range(2):
            mout_ref[sg, d:d + 1, :] = ms[d]
        return carry

    lax.fori_loop(0, nseg, scan_m, 0)

    def increments(g, carry):
        for u in range(MLSTM_GROUP):
            n = g * MLSTM_GROUP + u
            r = pl.multiple_of(n * L, L)
            k = k_ref[pl.ds(r, L), :]
            vb = v_ref[pl.ds(r, L), :].astype(BF16)
            gcol = gh_ref[pl.ds(r, L), :]
            for d in range(2):
                w_t = jnp.exp(sc_ref[d, 0, pl.ds(n, 1), 0:1] - gcol[:, 2 + d:3 + d] + gcol[:, d:d + 1]
                              - sc_ref[d, 3, pl.ds(n, 1), 0:1])
                kw = k * (w_t * scale)
                kv_ref[d, n] = _dot_tn(kw.astype(BF16), vb)
                ksum_ref[d, pl.ds(n, 1), :] = jnp.sum(kw, axis=0, keepdims=True)
        return carry

    lax.fori_loop(0, nct // MLSTM_GROUP, increments, 0)

    def scan_state(sg, carry):
        if has_init:
            st = (c0_ref[0], n0_ref[0, pl.ds(head, 1), :], c0_ref[1], n0_ref[1, pl.ds(head, 1), :])
        else:
            zc = jnp.zeros((HEAD_DIM, HEAD_DIM), F32)
            zn = jnp.zeros((1, HEAD_DIM), F32)
            st = (zc, zn, zc, zn)

        def step(c, st):
            out = []
            for d in range(2):
                cm, nv = st[2 * d], st[2 * d + 1]
                n = chunk_of(sg, c, d)
                cp_ref[d, n] = cm.astype(BF16)
                np_ref[d, pl.ds(n, 1), :] = nv
                w_s = jnp.exp(sc_ref[d, 0, pl.ds(n, 1), 0:1] + sc_ref[d, 2, pl.ds(n, 1), 0:1]
                              - sc_ref[d, 3, pl.ds(n, 1), 0:1])
                out += [w_s * cm + kv_ref[d, n], w_s * nv + ksum_ref[d, pl.ds(n, 1), :]]
            return tuple(out)

        st = lax.fori_loop(0, segc, step, st)
        for d in range(2):
            cout_ref[sg, d] = st[2 * d]
            nout_ref[sg, d:d + 1, :] = st[2 * d + 1]
        return carry

    lax.fori_loop(0, nseg, scan_state, 0)

    def outputs(g, carry):
        for u in range(MLSTM_GROUP):
            n = g * MLSTM_GROUP + u
            r = pl.multiple_of(n * L, L)
            q = q_ref[pl.ds(r, L), :]
            qb = q.astype(BF16)
            vb = v_ref[pl.ds(r, L), :].astype(BF16)
            qk = _dot_nt(qb, (k_ref[pl.ds(r, L), :] * scale).astype(BF16))
            gcol = gh_ref[pl.ds(r, L), :]
            for d in range(2):
                bcol = gcol[:, 2 + d:3 + d]
                m_prev = sc_ref[d, 2, pl.ds(n, 1), 0:1]
                logd = jnp.where(masks[d], bcol - rows_ref[2 + d, pl.ds(n, 1), :] + rows_ref[d, pl.ds(n, 1), :],
                                 NEG_BIG)
                m_i = jnp.maximum(bcol + m_prev, jnp.max(logd, axis=1, keepdims=True))
                s = qk * jnp.exp(logd - m_i)
                w_prev = jnp.exp(bcol + m_prev - m_i)
                num = w_prev * _dot(qb, cp_ref[d, n]) + _dot(s.astype(BF16), vb)
                den = (w_prev * jnp.sum(q * np_ref[d, pl.ds(n, 1), :], axis=1, keepdims=True)
                       + jnp.sum(s, axis=1, keepdims=True))
                h_refs[d][pl.ds(r, L), :] = num / jnp.maximum(jnp.abs(den), jnp.exp(-m_i))
        return carry

    lax.fori_loop(0, nct // MLSTM_GROUP, outputs, 0)
    y = _head_out(hf_ref[...] + hb_ref[...], hng_ref[...], _sigmoid(o_ref[...]))
    y_ref[...] = y.astype(y_ref.dtype)


def _ret_kernel(nseg, has_init, use_rope, *refs):
    L = CHUNK_C
    refs = list(refs)
    q_ref, k_ref, v_ref, g_ref, lg_ref, hng_ref = refs[:6]
    pos = 6
    if use_rope:
        cos_ref, sin_ref = refs[pos:pos + 2]
        pos += 2
    if has_init:
        s0_ref = refs[pos]
        pos += 2
    y_ref, sout_ref, o_ref, qs_ref, ks_ref, kv_ref, sp_ref = refs[pos:]
    rb = q_ref.shape[0]
    nct = rb // L
    segc = nct // nseg
    hd = HEAD_DIM

    q = q_ref[...] * HEAD_DIM ** -0.5
    k = k_ref[...]
    if use_rope:
        lane = lax.broadcasted_iota(jnp.int32, (rb, HEAD_DIM), 1)
        cos = cos_ref[...]
        sin = sin_ref[...]
        half = HEAD_DIM // 2

        def rope(x):
            rot = pltpu.roll(x, half, 1)
            return x * cos + jnp.where(lane < half, -rot, rot) * sin
        q = rope(q)
        k = rope(k)
    qs_ref[...] = q
    ks_ref[...] = k

    lg_f = lg_ref[0:1, 0:1]
    lg_b = lg_ref[1:2, 0:1]
    i, j = _tri_masks(L)
    dij = (i - j).astype(F32)
    decay_f = jnp.where(i >= j, jnp.exp(lg_f * jnp.maximum(dij, 0.0)), 0.0)
    decay_b = jnp.where(i <= j, jnp.exp(lg_b * jnp.maximum(-dij, 0.0)), 0.0)
    pos_c = lax.broadcasted_iota(jnp.int32, (L, 1), 0).astype(F32)
    qdec_f = jnp.exp(lg_f * (pos_c + 1.0))
    kend_f = jnp.exp(lg_f * (L - 1.0 - pos_c))
    qdec_b = jnp.exp(lg_b * (L - pos_c))
    kend_b = jnp.exp(lg_b * pos_c)
    cdec_f = jnp.exp(lg_f * L)
    cdec_b = jnp.exp(lg_b * L)

    decay = decay_f + decay_b

    def increments(g, carry):
        for u in range(RET_GROUP):
            n = g * RET_GROUP + u
            r = pl.multiple_of(n * L, L)
            kc = ks_ref[pl.ds(r, L), :]
            ke = jnp.concatenate([kc * kend_f, kc * kend_b], axis=1).astype(BF16)
            kv_ref[n] = _dot_tn(ke, v_ref[pl.ds(r, L), :].astype(BF16))
        return carry

    lax.fori_loop(0, nct // RET_GROUP, increments, 0)

    def scan_state(sg, carry):
        if has_init:
            st = (s0_ref[0], s0_ref[1])
        else:
            st = (jnp.zeros((hd, hd), F32),) * 2

        def step(c, st):
            nf = sg * segc + c
            nb = sg * segc + (segc - 1 - c)
            sp_ref[nf, 0:hd, :] = st[0].astype(BF16)
            sp_ref[nb, hd:2 * hd, :] = st[1].astype(BF16)
            return (cdec_f * st[0] + kv_ref[nf, 0:hd, :], cdec_b * st[1] + kv_ref[nb, hd:2 * hd, :])

        st = lax.fori_loop(0, segc, step, st)
        sout_ref[sg, 0] = st[0]
        sout_ref[sg, 1] = st[1]
        return carry

    lax.fori_loop(0, nseg, scan_state, 0)

    def outputs(g, carry):
        for u in range(RET_GROUP):
            n = g * RET_GROUP + u
            r = pl.multiple_of(n * L, L)
            qc = qs_ref[pl.ds(r, L), :]
            qk = _dot_nt(qc.astype(BF16), ks_ref[pl.ds(r, L), :].astype(BF16))
            inner = _dot((qk * decay).astype(BF16), v_ref[pl.ds(r, L), :].astype(BF16))
            qd = jnp.concatenate([qc * qdec_f, qc * qdec_b], axis=1).astype(BF16)
            o_ref[pl.ds(r, L), :] = inner + _dot(qd, sp_ref[n])
        return carry

    lax.fori_loop(0, nct // RET_GROUP, outputs, 0)
    gate = g_ref[...]
    y = _head_out(o_ref[...], hng_ref[...], gate * _sigmoid(gate))
    y_ref[...] = y.astype(y_ref.dtype)


def _retention(p, lgrows, hng, rope, y_prev, s0, *, blk0, nblk, nseg, layer, col0, rb=ROW_GROUP):
    m = p.shape[0]
    assert (rb // CHUNK_C) % RET_GROUP == 0
    colspec = lambda c0: pl.BlockSpec((rb, HEAD_DIM), lambda b, h: (blk0 + b, c0 + h))
    in_specs = [colspec(col0), colspec(col0 + 8), colspec(col0 + 16), colspec(col0 + 24),
                pl.BlockSpec((None, 8, HEAD_DIM), lambda b, h: (h, 0, 0)),
                pl.BlockSpec((1, HEAD_DIM), lambda b, h: (0, 0))]
    args = [p, p, p, p, lgrows, hng]
    if rope is not None:
        in_specs += [pl.BlockSpec((rb, HEAD_DIM), lambda b, h: (0, 0))] * 2
        args += list(rope)
    aliases = {}
    if s0 is not None:
        in_specs += [pl.BlockSpec((None, None, 2, None, HEAD_DIM, HEAD_DIM), lambda b, h: (b, layer, 0, h, 0, 0)),
                     pl.BlockSpec(memory_space=pl.ANY)]
        args += [s0, y_prev]
        aliases = {len(args) - 1: 0}
    nsq = nblk * nseg
    return pl.pallas_call(
        functools.partial(_ret_kernel, nseg, s0 is not None, rope is not None),
        grid=(nblk, N_HEADS),
        in_specs=in_specs,
        out_specs=[pl.BlockSpec((rb, HEAD_DIM), lambda b, h: (blk0 + b, h)),
                   pl.BlockSpec((nseg, 2, None, HEAD_DIM, HEAD_DIM), lambda b, h: (b, 0, h, 0, 0))],
        out_shape=[jax.ShapeDtypeStruct((m, W_MIX), BF16),
                   jax.ShapeDtypeStruct((nsq, 2, N_HEADS, HEAD_DIM, HEAD_DIM), F32)],
        scratch_shapes=[pltpu.VMEM((rb, HEAD_DIM), F32), pltpu.VMEM((rb, HEAD_DIM), F32),
                        pltpu.VMEM((rb, HEAD_DIM), F32),
                        pltpu.VMEM((rb // CHUNK_C, 2 * HEAD_DIM, HEAD_DIM), F32),
                        pltpu.VMEM((rb // CHUNK_C, 2 * HEAD_DIM, HEAD_DIM), BF16)],
        input_output_aliases=aliases,
        compiler_params=_params(("arbitrary", "arbitrary")),
        name="retention",
    )(*args)


def _inv_unit_triangular(mats, eye, i, j):
    n = eye.shape[0]
    xs = [eye - jnp.where((i >> 1) == (j >> 1), a, 0.0) for a in mats]
    s, sh = 2, 1
    while s < n:
        join = ((i >> (sh + 1)) == (j >> (sh + 1))) & ((i >> sh) != (j >> sh))
        xb = [x.astype(BF16) for x in xs]
        mt = [_dot(jnp.where(join, a, 0.0).astype(BF16), b).astype(BF16) for a, b in zip(mats, xb)]
        xs = [x - _dot(b, t) for x, b, t in zip(xs, xb, mt)]
        s, sh = 2 * s, sh + 1
    rs = [(eye - x - _dot_x3(a, x)).astype(BF16) for a, x in zip(mats, xs)]
    return [x + _dot(x.astype(BF16), r) for x, r in zip(xs, rs)]


def _delta_kernel(nseg, has_init, *refs):
    L = CHUNK_B
    refs = list(refs)
    q_ref, k_ref, v_ref, z_ref, cwq_ref, cwk_ref, cwv_ref, gc_ref, rows_ref, hng_ref = refs[:10]
    pos = 10
    if has_init:
        s0_ref = refs[pos]
        pos += 2
    (y_ref, sout_ref, qs_ref, ks_ref, vs_ref, gh_ref, u0f_ref, u0b_ref, wkf_ref, wkb_ref,
     qkf_ref, qkb_ref, qdf_ref, qdb_ref, ketf_ref, ketb_ref, sf_ref, sb_ref) = refs[pos:]
    of_ref, ob_ref = qs_ref, vs_ref
    rb = q_ref.shape[0]
    tseg = rb // nseg
    segc = tseg // L
    nchunks = rb // L

    row = lax.broadcasted_iota(jnp.int32, (rb, HEAD_DIM), 0)
    rowmod = row % tseg
    first = rowmod == 0
    last = rowmod == tseg - 1

    def conv_silu(x_ref, w_ref):
        x = x_ref[...]
        prev = jnp.where(first, 0.0, pltpu.roll(x, 1, 0))
        nxt = jnp.where(last, 0.0, pltpu.roll(x, rb - 1, 0))
        y = w_ref[0:1, :] * prev + w_ref[1:2, :] * x + w_ref[2:3, :] * nxt
        return y * _sigmoid(y)

    def l2n(x):
        return x * lax.rsqrt(jnp.sum(x * x, axis=-1, keepdims=True) + EPS)

    qs_ref[...] = l2n(conv_silu(q_ref, cwq_ref)) * HEAD_DIM ** -0.5
    ks_ref[...] = l2n(conv_silu(k_ref, cwk_ref))
    vs_ref[...] = conv_silu(v_ref, cwv_ref)
    gh_ref[...] = _select_cols(gc_ref[...], 32)

    i, j = _tri_masks(L)
    eye = (i == j).astype(F32)
    masks = ((i >= j, i > j), (i <= j, i < j))

    dir_refs = ((u0f_ref, wkf_ref, qkf_ref, qdf_ref, ketf_ref), (u0b_ref, wkb_ref, qkb_ref, qdb_ref, ketb_ref))

    def g_last(rw, d):
        return rw[2:3, L - 1:L] if d == 0 else rw[3:4, 0:1]

    def prep(gi, carry):
        chains = []
        for u in range(PREP_GROUP):
            c = gi * PREP_GROUP + u
            r = pl.multiple_of(c * L, L)
            qc = qs_ref[pl.ds(r, L), :]
            kc = ks_ref[pl.ds(r, L), :]
            kb = kc.astype(BF16)
            kk = _dot_nt(kb, kb)
            qk = _dot_nt(qc.astype(BF16), kb)
            gcol = gh_ref[pl.ds(r, L), :]
            rw = rows_ref[c]
            for d in range(2):
                incl, strict = masks[d]
                g_c = gcol[:, 2 + d:3 + d]
                decay = jnp.where(incl, jnp.exp(jnp.where(incl, g_c - rw[2 + d:3 + d, :], 0.0)), 0.0)
                dir_refs[d][2][pl.ds(r, L), :] = (qk * decay).astype(BF16)
                dir_refs[d][3][pl.ds(r, L), :] = (qc * jnp.exp(g_c)).astype(BF16)
                dir_refs[d][4][c] = (kc * jnp.exp(g_last(rw, d) - g_c)).T.astype(BF16)
                chains.append((r, d, jnp.where(strict, kk * decay, 0.0) * gcol[:, d:d + 1]))
        invs = _inv_unit_triangular([a for _, _, a in chains], eye, i, j)
        for (r, d, _), t in zip(chains, invs):
            gcol = gh_ref[pl.ds(r, L), :]
            beta_c = gcol[:, d:d + 1]
            bg = beta_c * jnp.exp(gcol[:, 2 + d:3 + d])
            tb = t.astype(BF16)
            dir_refs[d][0][pl.ds(r, L), :] = _dot(tb, (beta_c * vs_ref[pl.ds(r, L), :]).astype(BF16))
            dir_refs[d][1][pl.ds(r, L), :] = _dot(tb, (bg * ks_ref[pl.ds(r, L), :]).astype(BF16)).astype(BF16)
        return carry

    lax.fori_loop(0, nchunks // PREP_GROUP, prep, 0)

    def one(c, d, s_ref, o_ref):
        u0_ref, wk_ref, qkd_ref, qd_ref, ket_ref = dir_refs[d]
        r = pl.multiple_of(c * L, L)
        s_prev = s_ref[...]
        sb = s_prev.astype(BF16)
        ub = (u0_ref[pl.ds(r, L), :] - _dot(wk_ref[pl.ds(r, L), :], sb)).astype(BF16)
        o_ref[pl.ds(r, L), :] = _dot(qd_ref[pl.ds(r, L), :], sb) + _dot(qkd_ref[pl.ds(r, L), :], ub)
        s_ref[...] = jnp.exp(g_last(rows_ref[c], d)) * s_prev + _dot(ket_ref[c], ub)

    def segment(sg, carry):
        if has_init:
            sf_ref[...] = s0_ref[0]
            sb_ref[...] = s0_ref[1]
        else:
            sf_ref[...] = jnp.zeros((HEAD_DIM, HEAD_DIM), F32)
            sb_ref[...] = jnp.zeros((HEAD_DIM, HEAD_DIM), F32)

        def step(c, carry2):
            one(sg * segc + c, 0, sf_ref, of_ref)
            one(sg * segc + (segc - 1 - c), 1, sb_ref, ob_ref)
            return carry2

        lax.fori_loop(0, segc, step, 0)
        sout_ref[sg, 0] = sf_ref[...]
        sout_ref[sg, 1] = sb_ref[...]
        return carry

    lax.fori_loop(0, nseg, segment, 0)
    gate = z_ref[...]
    y = _head_out(of_ref[...] + ob_ref[...], hng_ref[...], gate * _sigmoid(gate))
    y_ref[...] = y.astype(y_ref.dtype)


def _delta(p, conv_w, gc, rows, hng, y_prev, s0, *, blk0, nblk, nseg, layer, col0, rb=ROW_GROUP):
    m = p.shape[0]
    cb = rb // CHUNK_B
    assert cb % PREP_GROUP == 0
    colspec = lambda c0: pl.BlockSpec((rb, HEAD_DIM), lambda b, h: (blk0 + b, c0 + h))
    cwspec = lambda c0: pl.BlockSpec((None, CONV_K, HEAD_DIM), lambda b, h: (layer, 0, c0 + h))
    in_specs = [colspec(col0), colspec(col0 + 8), colspec(col0 + 16), colspec(col0 + 24),
                cwspec(0), cwspec(8), cwspec(16),
                pl.BlockSpec((rb, GATE_LANES), lambda b, h: (blk0 + b, 0)),
                pl.BlockSpec((None, cb, 8, CHUNK_B), lambda b, h: (h, blk0 + b, 0, 0)),
                pl.BlockSpec((1, HEAD_DIM), lambda b, h: (0, 0))]
    args = [p, p, p, p, conv_w, conv_w, conv_w, gc, rows, hng]
    aliases = {}
    if s0 is not None:
        in_specs += [pl.BlockSpec((None, None, 2, None, HEAD_DIM, HEAD_DIM), lambda b, h: (b, layer, 0, h, 0, 0)),
                     pl.BlockSpec(memory_space=pl.ANY)]
        args += [s0, y_prev]
        aliases = {len(args) - 1: 0}
    nsq = nblk * nseg
    big = lambda dt: pltpu.VMEM((rb, HEAD_DIM), dt)
    return pl.pallas_call(
        functools.partial(_delta_kernel, nseg, s0 is not None),
        grid=(nblk, N_HEADS),
        in_specs=in_specs,
        out_specs=[pl.BlockSpec((rb, HEAD_DIM), lambda b, h: (blk0 + b, h)),
                   pl.BlockSpec((nseg, 2, None, HEAD_DIM, HEAD_DIM), lambda b, h: (b, 0, h, 0, 0))],
        out_shape=[jax.ShapeDtypeStruct((m, W_MIX), BF16),
                   jax.ShapeDtypeStruct((nsq, 2, N_HEADS, HEAD_DIM, HEAD_DIM), F32)],
        scratch_shapes=[big(F32), big(F32), big(F32), pltpu.VMEM((rb, GATE_LANES), F32),
                        big(F32), big(F32), big(BF16), big(BF16),
                        pltpu.VMEM((rb, CHUNK_B), BF16), pltpu.VMEM((rb, CHUNK_B), BF16),
                        big(BF16), big(BF16),
                        pltpu.VMEM((cb, HEAD_DIM, CHUNK_B), BF16), pltpu.VMEM((cb, HEAD_DIM, CHUNK_B), BF16),
                        pltpu.VMEM((HEAD_DIM, HEAD_DIM), F32), pltpu.VMEM((HEAD_DIM, HEAD_DIM), F32)],
        input_output_aliases=aliases,
        compiler_params=_params(("arbitrary", "arbitrary")),
        name="deltanet",
    )(*args)


def _state_specs(nseg):
    specs = [pl.BlockSpec((nseg, 2, None, HEAD_DIM, HEAD_DIM), lambda b, h: (b, 0, h, 0, 0)),
             pl.BlockSpec((None, nseg, 2, HEAD_DIM), lambda b, h: (h, b, 0, 0)),
             pl.BlockSpec((None, nseg, 2, HEAD_DIM), lambda b, h: (h, b, 0, 0))]
    return specs


def _mlstm(p, gc, rows, hng, y_prev, init, *, blk0, nblk, nseg, layer, rb=ROW_GROUP):
    m = p.shape[0]
    cb = rb // CHUNK_A
    assert cb % MLSTM_GROUP == 0
    colspec = lambda c0: pl.BlockSpec((rb, HEAD_DIM), lambda b, h: (blk0 + b, c0 + h))
    in_specs = [colspec(0), colspec(8), colspec(16), colspec(24),
                pl.BlockSpec((rb, GATE_LANES), lambda b, h: (blk0 + b, 0)),
                pl.BlockSpec((None, 8, cb, CHUNK_A), lambda b, h: (h, 0, blk0 + b, 0)),
                pl.BlockSpec((1, HEAD_DIM), lambda b, h: (0, 0))]
    args = [p, p, p, p, gc, rows, hng]
    if init is not None:
        c0, n0, m0 = init
        in_specs += [pl.BlockSpec((None, None, 2, None, HEAD_DIM, HEAD_DIM), lambda b, h: (b, layer, 0, h, 0, 0)),
                     pl.BlockSpec((None, None, 2, N_HEADS, HEAD_DIM), lambda b, h: (b, layer, 0, 0, 0)),
                     pl.BlockSpec((None, None, 2, N_HEADS, HEAD_DIM), lambda b, h: (b, layer, 0, 0, 0)),
                     pl.BlockSpec(memory_space=pl.ANY)]
        args += [c0, n0, m0, y_prev]
        aliases = {len(args) - 1: 0}
    else:
        aliases = {}
    nsq = nblk * nseg
    out_shape = [jax.ShapeDtypeStruct((m, W_MIX), BF16),
                 jax.ShapeDtypeStruct((nsq, 2, N_HEADS, HEAD_DIM, HEAD_DIM), F32),
                 jax.ShapeDtypeStruct((N_HEADS, nsq, 2, HEAD_DIM), F32),
                 jax.ShapeDtypeStruct((N_HEADS, nsq, 2, HEAD_DIM), F32)]
    out_specs = [pl.BlockSpec((rb, HEAD_DIM), lambda b, h: (blk0 + b, h))] + _state_specs(nseg)
    return pl.pallas_call(
        functools.partial(_mlstm_kernel, nseg, init is not None),
        grid=(nblk, N_HEADS),
        in_specs=in_specs, out_specs=out_specs, out_shape=out_shape,
        scratch_shapes=[pltpu.VMEM((rb, HEAD_DIM), F32), pltpu.VMEM((rb, HEAD_DIM), F32),
                        pltpu.VMEM((rb, GATE_LANES), F32),
                        pltpu.VMEM((2, 4, cb, HEAD_DIM), F32),
                        pltpu.VMEM((2, cb, HEAD_DIM, HEAD_DIM), F32), pltpu.VMEM((2, cb, HEAD_DIM), F32),
                        pltpu.VMEM((2, cb, HEAD_DIM, HEAD_DIM), BF16), pltpu.VMEM((2, cb, HEAD_DIM), F32)],
        input_output_aliases=aliases,
        compiler_params=_params(("arbitrary", "arbitrary")),
        name="mlstm",
    )(*args)


_MAIN_COLS = ((0, 4 * W_MIX), (4 * W_MIX + 32, 8 * W_MIX + 32), (8 * W_MIX + 64, 12 * W_MIX + 64 + 3 * D_MODEL))
_GATE_COLS = ((4 * W_MIX, 4 * W_MIX + 32), (8 * W_MIX + 32, 8 * W_MIX + 64))
COL_A, COL_B, COL_C = 0, 32, 64


def _take_cols(w, ranges):
    return jnp.concatenate([w[..., a:b] for a, b in ranges], axis=-1)


def _rope_tables(t):
    pos = jnp.arange(t)
    pos_r = (pos // GRID_W).astype(F32)
    pos_c = (pos % GRID_W).astype(F32)
    nf = HEAD_DIM // 4
    freqs = ROPE_BASE ** (-jnp.arange(nf, dtype=F32) / nf)
    ang = jnp.concatenate([pos_r[:, None] * freqs, pos_c[:, None] * freqs], axis=-1)
    ang = jnp.concatenate([ang, ang], axis=-1)
    return jnp.cos(ang), jnp.sin(ang)


def kernel(x_prompt, x_sample, state_mlstm_C, state_mlstm_n, state_mlstm_m, state_delta_S, state_ret_S,
           c, c_ctx, norm_g, final_norm_g, w_ada, b_ada, w_in, b_in, mlstm_f_bias, conv_w, delta_A_log,
           delta_dt_bias, ret_log_gamma, head_norm_g, w_br, w_out, ffn_w13, ffn_w2):
    bp, sp, d = x_prompt.shape
    bs, ss, _ = x_sample.shape
    assert d == D_MODEL and bp * sp == ROW_GROUP and ss == ROW_GROUP
    n_prompt_blk, n_sample_blk = 1, bs
    x = jnp.concatenate([x_prompt.reshape(-1, d), x_sample.reshape(-1, d)], axis=0)

    cv8 = jnp.zeros((8, d), F32).at[0].set(c_ctx).at[1:1 + bs].set(c)
    mod = _ada(cv8, w_ada, b_ada)[:, :1 + bs].reshape(DEPTH, 1 + bs, N_MOD, 1, d)
    rope = _rope_tables(ss)
    m0_pad = jnp.broadcast_to(state_mlstm_m[..., None], state_mlstm_m.shape + (HEAD_DIM,))

    w_main = _take_cols(w_in, _MAIN_COLS).astype(BF16)
    b_main = _take_cols(b_in, _MAIN_COLS)
    pad = GATE_LANES - 64
    w_gate = jnp.pad(_take_cols(w_in, _GATE_COLS), ((0, 0), (0, 0), (0, pad))).astype(BF16)
    b_gate = jnp.pad(_take_cols(b_in, _GATE_COLS), ((0, 0), (0, pad)))

    new_c, new_n, new_m, new_sd, new_sr = [], [], [], [], []
    for l in range(DEPTH):
        md = lambda i: mod[l, :, i]
        h = _normmod(x, norm_g[l, 0], md(0), md(1))
        x = _mm_res(_mm_swiglu(h, ffn_w13, (l, 0)), ffn_w2, (l, 0), x, md(2), 0.5, 1024, 512)

        h = _normmod(x, norm_g[l, 1], md(3), md(4))
        p = _mm_bias(h, w_main, b_main, l, 1024)
        graw = _mm_bias(h, w_gate, b_gate, l, GATE_LANES)
        pv = jnp.zeros((8, GATE_LANES), F32)
        pv = pv.at[0, 16:32].set(mlstm_f_bias[l].reshape(-1))
        pv = pv.at[1, 48:64].set(delta_dt_bias[l].reshape(-1))
        pv = pv.at[2, 48:64].set(delta_A_log[l].reshape(-1))
        gc, grt = _gateprep(graw, pv)
        rows_a = _gate_rows(grt, 0, CHUNK_A, chunk_major=False)
        rows_b = _gate_rows(grt, 32, CHUNK_B)
        hng = head_norm_g[l].reshape(3, 1, HEAD_DIM)
        lgrows = jnp.zeros((N_HEADS, 8, HEAD_DIM), F32).at[:, 0:2, :].set(
            jnp.broadcast_to(ret_log_gamma[l].T[:, :, None], (N_HEADS, 2, HEAD_DIM)))
        prompt = dict(blk0=0, nblk=n_prompt_blk, nseg=bp, layer=l)
        sample = dict(blk0=n_prompt_blk, nblk=n_sample_blk, nseg=1, layer=l)

        ya, c_new, n_new, m_new = _mlstm(p, gc, rows_a, hng[0], None, None, **prompt)
        ya = _mlstm(p, gc, rows_a, hng[0], ya, (state_mlstm_C, state_mlstm_n, m0_pad), **sample)[0]
        yb, sd_new = _delta(p, conv_w, gc, rows_b, hng[1], None, None, col0=COL_B, **prompt)
        yb = _delta(p, conv_w, gc, rows_b, hng[1], yb, state_delta_S, col0=COL_B, **sample)[0]
        yc, sr_new = _retention(p, lgrows, hng[2], None, None, None, col0=COL_C, **prompt)
        yc = _retention(p, lgrows, hng[2], rope, yc, state_ret_S, col0=COL_C, **sample)[0]
        new_c.append(c_new)
        new_n.append(jnp.transpose(n_new, (1, 2, 0, 3)))
        new_m.append(jnp.transpose(m_new[..., 0], (1, 2, 0)))
        new_sd.append(sd_new)
        new_sr.append(sr_new)

        x = _mm_res(_merge(ya, yb, yc, w_br, l, p), w_out, (l,), x, md(5), 1.0, 1024, 1024)
        h = _normmod(x, norm_g[l, 2], md(6), md(7))
        x = _mm_res(_mm_swiglu(h, ffn_w13, (l, 1)), ffn_w2, (l, 1), x, md(8), 0.5, 1024, 512)

    y_prompt = _final_norm(x, final_norm_g, 0, bp * sp).reshape(bp, sp, d)
    y_sample = _final_norm(x, final_norm_g, bp * sp, bs * ss).reshape(bs, ss, d)
    stack = lambda xs: jnp.stack(xs, axis=1)
    return (y_prompt, y_sample, stack(new_c), stack(new_n), stack(new_m), stack(new_sd), stack(new_sr))
```

```python
import functools

import jax
import jax.numpy as jnp
import numpy as np
from jax import lax
from jax.experimental import pallas as pl
from jax.experimental.pallas import tpu as pltpu

F32 = jnp.float32
BF16 = jnp.bfloat16

D_MODEL = 2048
DEPTH = 2
N_HEADS = 8
HEAD_DIM = 128
W_MIX = N_HEADS * HEAD_DIM
D_FF = 4096
N_MOD = 9
CONV_K = 3
GRID_W = 64
ROPE_BASE = 10000.0
EPS = 1e-6
ROW_GROUP = 4096
N_MAIN = 12 * W_MIX + 3 * D_MODEL
GATE_LANES = 128
CHUNK_A = 128
CHUNK_B = 128
RET_GROUP = 4
MLSTM_GROUP = 2
PREP_GROUP = 8
CHUNK_C = 128
NEG_BIG = -1e30
VMEM_LIMIT = 56 * 1024 * 1024


def _params(sem):
    return pltpu.CompilerParams(dimension_semantics=sem, vmem_limit_bytes=VMEM_LIMIT)


def _dot(a, b):
    return jnp.dot(a, b, preferred_element_type=F32)


def _dot_nt(a, b):
    return lax.dot_general(a, b, (((1,), (1,)), ((), ())), preferred_element_type=F32)


def _dot_tn(a, b):
    return lax.dot_general(a, b, (((0,), (0,)), ((), ())), preferred_element_type=F32)


def _split3(x):
    hi = x.astype(BF16)
    r1 = x - hi.astype(F32)
    mid = r1.astype(BF16)
    lo = (r1 - mid.astype(F32)).astype(BF16)
    return hi, mid, lo


def _dot_exact_rhs01(x, sel):
    hi, mid, lo = _split3(x)
    return _dot(hi, sel) + _dot(mid, sel) + _dot(lo, sel)


def _dot_x3(a, b):
    ah = a.astype(BF16)
    al = (a - ah.astype(F32)).astype(BF16)
    bh = b.astype(BF16)
    bl = (b - bh.astype(F32)).astype(BF16)
    return _dot(ah, bh) + _dot(ah, bl) + _dot(al, bh)


def _sigmoid(x):
    return 0.5 * jnp.tanh(0.5 * x) + 0.5


def _softplus(x):
    return jnp.maximum(x, 0.0) + jnp.log(1.0 + jnp.exp(-jnp.abs(x)))


def _ada_kernel(cv_ref, w_ref, b_ref, o_ref):
    cv = cv_ref[...]
    s = (cv * _sigmoid(cv)).astype(BF16)
    o_ref[...] = _dot(s, w_ref[...].astype(BF16)) + b_ref[...]


def _ada(cv8, w_ada, b_ada):
    n = w_ada.shape[-1]
    bn = 1024
    return pl.pallas_call(
        _ada_kernel,
        grid=(DEPTH, n // bn),
        in_specs=[pl.BlockSpec((8, D_MODEL), lambda l, j: (0, 0)),
                  pl.BlockSpec((None, D_MODEL, bn), lambda l, j: (l, 0, j)),
                  pl.BlockSpec((None, 1, bn), lambda l, j: (l, 0, j))],
        out_specs=pl.BlockSpec((None, 8, bn), lambda l, j: (l, 0, j)),
        out_shape=jax.ShapeDtypeStruct((DEPTH, 8, n), F32),
        compiler_params=_params(("arbitrary", "arbitrary")),
        name="adaln",
    )(cv8, w_ada, b_ada.reshape(DEPTH, 1, n))


def _normmod_kernel(x_ref, g_ref, sh_ref, sc_ref, o_ref):
    x = x_ref[...]
    y = x * lax.rsqrt(jnp.mean(x * x, axis=-1, keepdims=True) + EPS) * g_ref[...]
    o_ref[...] = (y * (1.0 + sc_ref[...]) + sh_ref[...]).astype(o_ref.dtype)


def _normmod(x, g, shift, scale):
    m = x.shape[0]
    bm = 256
    grp = ROW_GROUP // bm
    return pl.pallas_call(
        _normmod_kernel,
        grid=(m // bm,),
        in_specs=[pl.BlockSpec((bm, D_MODEL), lambda i: (i, 0)),
                  pl.BlockSpec((1, D_MODEL), lambda i: (0, 0)),
                  pl.BlockSpec((None, 1, D_MODEL), lambda i: (i // grp, 0, 0)),
                  pl.BlockSpec((None, 1, D_MODEL), lambda i: (i // grp, 0, 0))],
        out_specs=pl.BlockSpec((bm, D_MODEL), lambda i: (i, 0)),
        out_shape=jax.ShapeDtypeStruct((m, D_MODEL), BF16),
        compiler_params=_params(("arbitrary",)),
        name="normmod",
    )(x, g.reshape(1, D_MODEL), shift, scale)


def _rmsnorm_kernel(x_ref, g_ref, o_ref):
    x = x_ref[...]
    o_ref[...] = x * lax.rsqrt(jnp.mean(x * x, axis=-1, keepdims=True) + EPS) * g_ref[...]


def _final_norm(x, g, row0, rows):
    bm = 256
    off = row0 // bm
    return pl.pallas_call(
        _rmsnorm_kernel,
        grid=(rows // bm,),
        in_specs=[pl.BlockSpec((bm, D_MODEL), lambda i: (i + off, 0)),
                  pl.BlockSpec((1, D_MODEL), lambda i: (0, 0))],
        out_specs=pl.BlockSpec((bm, D_MODEL), lambda i: (i, 0)),
        out_shape=jax.ShapeDtypeStruct((rows, D_MODEL), F32),
        compiler_params=_params(("arbitrary",)),
        name="final_norm",
    )(x, g.reshape(1, D_MODEL))


def _swiglu_kernel(h_ref, wa_ref, wb_ref, o_ref, was, wbs):
    @pl.when(pl.program_id(1) == 0)
    def _():
        was[...] = wa_ref[...].astype(BF16)
        wbs[...] = wb_ref[...].astype(BF16)
    h = h_ref[...]
    a = _dot(h, was[...])
    b = _dot(h, wbs[...])
    o_ref[...] = (a * _sigmoid(a) * b).astype(o_ref.dtype)


def _wspec(widx, k, bn, col=lambda j: j):
    lead = tuple(widx)
    return pl.BlockSpec((None,) * len(lead) + (k, bn), lambda j, i: lead + (0, col(j)))


def _mm_swiglu(h, w13, widx):
    m, k = h.shape
    f = w13.shape[-1] // 2
    bm, bn = 1024, 512
    nb = f // bn
    return pl.pallas_call(
        _swiglu_kernel,
        grid=(nb, m // bm),
        in_specs=[pl.BlockSpec((bm, k), lambda j, i: (i, 0)),
                  _wspec(widx, k, bn),
                  _wspec(widx, k, bn, lambda j: j + nb)],
        out_specs=pl.BlockSpec((bm, bn), lambda j, i: (i, j)),
        out_shape=jax.ShapeDtypeStruct((m, f), BF16),
        scratch_shapes=[pltpu.VMEM((k, bn), BF16), pltpu.VMEM((k, bn), BF16)],
        compiler_params=_params(("arbitrary", "arbitrary")),
        name="ffn_up",
    )(h, w13, w13)


def _mm_res_kernel(coef, a_ref, w_ref, x_ref, s_ref, o_ref, ws):
    @pl.when(pl.program_id(1) == 0)
    def _():
        ws[...] = w_ref[...].astype(BF16)
    acc = _dot(a_ref[...], ws[...])
    o_ref[...] = x_ref[...] + (coef * s_ref[...]) * acc


def _mm_res(a, w, widx, x, s, coef, bm, bn):
    m, k = a.shape
    n = w.shape[-1]
    grp = ROW_GROUP // bm
    return pl.pallas_call(
        functools.partial(_mm_res_kernel, coef),
        grid=(n // bn, m // bm),
        in_specs=[pl.BlockSpec((bm, k), lambda j, i: (i, 0)),
                  _wspec(widx, k, bn),
                  pl.BlockSpec((bm, bn), lambda j, i: (i, j)),
                  pl.BlockSpec((None, 1, bn), lambda j, i: (i // grp, 0, j))],
        out_specs=pl.BlockSpec((bm, bn), lambda j, i: (i, j)),
        out_shape=jax.ShapeDtypeStruct((m, n), F32),
        scratch_shapes=[pltpu.VMEM((k, bn), BF16)],
        compiler_params=_params(("arbitrary", "arbitrary")),
        name="mm_residual",
    )(a, w, x, s)


def _mm_bias_kernel(a_ref, w_ref, b_ref, o_ref):
    o_ref[...] = _dot(a_ref[...], w_ref[...]) + b_ref[...]


def _mm_bias(a, w, b, layer, bn):
    m, k = a.shape
    n = w.shape[-1]
    bm = 1024
    return pl.pallas_call(
        _mm_bias_kernel,
        grid=(n // bn, m // bm),
        in_specs=[pl.BlockSpec((bm, k), lambda j, i: (i, 0)),
                  _wspec((layer,), k, bn),
                  pl.BlockSpec((None, 1, bn), lambda j, i: (layer, 0, j))],
        out_specs=pl.BlockSpec((bm, bn), lambda j, i: (i, j)),
        out_shape=jax.ShapeDtypeStruct((m, n), F32),
        compiler_params=_params(("arbitrary", "arbitrary")),
        name="in_proj",
    )(a, w, b.reshape(DEPTH, 1, n))


IN_BN = 1024
_IN_SEGMENTS = ((0, 4, 0), (4, 8, 32), (8, 18, 64))


def _in_proj_kernel(a_ref, wa_ref, wb_ref, b_ref, o_ref, ws):
    j = pl.program_id(0)
    first_row_tile = pl.program_id(1) == 0
    k = wa_ref.shape[0]
    rows = 256
    for lo, hi, shift in _IN_SEGMENTS:
        @pl.when(first_row_tile & (j >= lo) & (j < hi))
        def _(shift=shift):
            for r in range(0, k, rows):
                w = wa_ref[r:r + rows, :]
                if shift:
                    w = jnp.concatenate([w, wb_ref[r:r + rows, :]], axis=1)[:, shift:shift + IN_BN]
                ws[r:r + rows, :] = w.astype(BF16)
    o_ref[...] = (_dot(a_ref[...], ws[...]) + b_ref[...]).astype(o_ref.dtype)


def _in_proj(a, w_in, b_main, layer):
    m, k = a.shape
    bm, bn = 1024, IN_BN
    assert N_MAIN // bn == _IN_SEGMENTS[-1][1]
    return pl.pallas_call(
        _in_proj_kernel,
        grid=(N_MAIN // bn, m // bm),
        in_specs=[pl.BlockSpec((bm, k), lambda j, i: (i, 0)),
                  pl.BlockSpec((None, k, bn), lambda j, i: (layer, 0, j)),
                  pl.BlockSpec((None, k, 128), lambda j, i: (layer, 0, (j + 1) * (bn // 128))),
                  pl.BlockSpec((None, 1, bn), lambda j, i: (layer, 0, j))],
        out_specs=pl.BlockSpec((bm, bn), lambda j, i: (i, j)),
        out_shape=jax.ShapeDtypeStruct((m, N_MAIN), BF16),
        scratch_shapes=[pltpu.VMEM((k, bn), BF16)],
        compiler_params=_params(("arbitrary", "arbitrary")),
        name="in_proj",
    )(a, w_in, w_in, b_main.reshape(DEPTH, 1, N_MAIN))


def _merge_kernel(ya_ref, yb_ref, yc_ref, w_ref, ga_ref, gb_ref, gc_ref, o_ref, ws):
    @pl.when(pl.program_id(1) == 0)
    def _():
        ws[...] = w_ref[...].astype(BF16)
    acc = _sigmoid(ga_ref[...].astype(F32)) * _dot(ya_ref[...], ws[0])
    acc = acc + _sigmoid(gb_ref[...].astype(F32)) * _dot(yb_ref[...], ws[1])
    acc = acc + _sigmoid(gc_ref[...].astype(F32)) * _dot(yc_ref[...], ws[2])
    o_ref[...] = acc.astype(o_ref.dtype)


def _merge(ya, yb, yc, w_br, layer, p):
    m = ya.shape[0]
    bm, bn = 512, 1024
    g0 = 12 * W_MIX // bn
    gstep = D_MODEL // bn
    yspec = pl.BlockSpec((bm, W_MIX), lambda j, i: (i, 0))
    return pl.pallas_call(
        _merge_kernel,
        grid=(D_MODEL // bn, m // bm),
        in_specs=[yspec, yspec, yspec,
                  pl.BlockSpec((None, 3, W_MIX, bn), lambda j, i: (layer, 0, 0, j)),
                  pl.BlockSpec((bm, bn), lambda j, i: (i, g0 + j)),
                  pl.BlockSpec((bm, bn), lambda j, i: (i, g0 + gstep + j)),
                  pl.BlockSpec((bm, bn), lambda j, i: (i, g0 + 2 * gstep + j))],
        out_specs=pl.BlockSpec((bm, bn), lambda j, i: (i, j)),
        out_shape=jax.ShapeDtypeStruct((m, D_MODEL), BF16),
        scratch_shapes=[pltpu.VMEM((3, W_MIX, bn), BF16)],
        compiler_params=_params(("arbitrary", "arbitrary")),
        name="branch_merge",
    )(ya, yb, yc, w_br, p, p, p)


def _gateprep_kernel(g_ref, pv_ref, gc_ref, grt_ref):
    bm = g_ref.shape[0]
    raw = g_ref[...]
    lane = lax.broadcasted_iota(jnp.int32, (bm, GATE_LANES), 1)
    row = lax.broadcasted_iota(jnp.int32, (bm, GATE_LANES), 0)
    fb = pv_ref[0:1, :]
    dtb = pv_ref[1:2, :]
    alog = pv_ref[2:3, :]
    lf = -_softplus(-(raw + fb))
    beta = _sigmoid(raw)
    gdec = -jnp.exp(alog) * _softplus(raw + dtb)
    act = jnp.where(lane < 16, raw, jnp.where(lane < 32, lf, jnp.where(lane < 48, beta, gdec)))
    seglen = jnp.where(lane < 32, CHUNK_A, CHUNK_B)
    rowmod = row & (seglen - 1)
    pre = act
    suf = act
    s = 1
    while s < max(CHUNK_A, CHUNK_B):
        pre = pre + jnp.where(rowmod >= s, pltpu.roll(pre, s, 0), 0.0)
        suf = suf + jnp.where(rowmod < seglen - s, pltpu.roll(suf, bm - s, 0), 0.0)
        s *= 2
    is_cum = ((lane >= 16) & (lane < 32)) | ((lane >= 48) & (lane < 64))
    backward = ((lane >> 3) & 1) == 1
    out = jnp.where(is_cum, jnp.where(backward, suf, pre), act)
    gc_ref[...] = out
    for t in range(bm // 128):
        grt_ref[:, t * 128:(t + 1) * 128] = out[t * 128:(t + 1) * 128, :].T


def _gateprep(graw, pv):
    m = graw.shape[0]
    bm = 512
    return pl.pallas_call(
        _gateprep_kernel,
        grid=(m // bm,),
        in_specs=[pl.BlockSpec((bm, GATE_LANES), lambda i: (i, 0)),
                  pl.BlockSpec((8, GATE_LANES), lambda i: (0, 0))],
        out_specs=[pl.BlockSpec((bm, GATE_LANES), lambda i: (i, 0)),
                   pl.BlockSpec((GATE_LANES, bm), lambda i: (0, i))],
        out_shape=[jax.ShapeDtypeStruct((m, GATE_LANES), F32),
                   jax.ShapeDtypeStruct((GATE_LANES, m), F32)],
        compiler_params=_params(("arbitrary",)),
        name="gate_prep",
    )(graw, pv)


def _gate_rows(grt, lane0, chunk, chunk_major=True):
    m = grt.shape[1]
    g = grt[lane0:lane0 + 32].reshape(4, N_HEADS, m // chunk, chunk)
    if chunk_major:
        return jnp.pad(jnp.transpose(g, (1, 2, 0, 3)), ((0, 0), (0, 0), (0, 4), (0, 0)))
    return jnp.pad(jnp.transpose(g, (1, 0, 2, 3)), ((0, 0), (0, 4), (0, 0), (0, 0)))


def _select_cols(gc, lane0):
    h = pl.program_id(1)
    src = lax.broadcasted_iota(jnp.int32, (GATE_LANES, GATE_LANES), 0)
    c = lax.broadcasted_iota(jnp.int32, (GATE_LANES, GATE_LANES), 1)
    sel = ((src == lane0 + 8 * c + h) & (c < 4)).astype(BF16)
    return _dot_exact_rhs01(gc, sel)


def _head_out(hsum, g, gate):
    y = hsum * lax.rsqrt(jnp.mean(hsum * hsum, axis=-1, keepdims=True) + EPS) * g
    return y * gate


def _tri_masks(n):
    i = lax.broadcasted_iota(jnp.int32, (n, n), 0)
    j = lax.broadcasted_iota(jnp.int32, (n, n), 1)
    return i, j


def _mlstm_kernel(nseg, has_init, *refs):
    L = CHUNK_A
    refs = list(refs)
    q_ref, k_ref, v_ref, o_ref, gc_ref, rows_ref, hng_ref = refs[:7]
    pos = 7
    if has_init:
        c0_ref, n0_ref, m0_ref = refs[pos:pos + 3]
        pos += 4
    (y_ref, cout_ref, nout_ref, mout_ref,
     hf_ref, hb_ref, gh_ref, sc_ref, kv_ref, ksum_ref, cp_ref, ce_ref) = refs[pos:]
    h_refs = (hf_ref, hb_ref)
    rb = q_ref.shape[0]
    nct = rb // L
    segc = nct // nseg
    head = pl.program_id(1)
    hd = HEAD_DIM
    scale = HEAD_DIM ** -0.5
    ones = jnp.ones((L, hd), BF16)
    gh_ref[...] = _select_cols(gc_ref[...], 0)
    i, j = _tri_masks(L)
    masks = (i >= j, i <= j)
    lanes = (nct, HEAD_DIM)
    for d in range(2):
        ig = rows_ref[d]
        b = rows_ref[2 + d]
        bl = b[:, L - 1:L] if d == 0 else b[:, 0:1]
        sc_ref[d, 0] = jnp.broadcast_to(bl, lanes)
        sc_ref[d, 1] = jnp.broadcast_to(jnp.max(bl - b + ig, axis=1, keepdims=True), lanes)

    def chunk_of(sg, c, d):
        return sg * segc + (c if d == 0 else segc - 1 - c)

    def scan_m(sg, carry):
        if has_init:
            ms = tuple(m0_ref[d, pl.ds(head, 1), :] for d in range(2))
        else:
            ms = (jnp.zeros((1, HEAD_DIM), F32),) * 2

        def step(c, ms):
            out = []
            for d in range(2):
                n = chunk_of(sg, c, d)
                sc_ref[d, 2, pl.ds(n, 1), :] = ms[d]
                m_new = jnp.maximum(sc_ref[d, 0, pl.ds(n, 1), :] + ms[d], sc_ref[d, 1, pl.ds(n, 1), :])
                sc_ref[d, 3, pl.ds(n, 1), :] = m_new
                out.append(m_new)
            return tuple(out)

        ms = lax.fori_loop(0, segc, step, ms)
        for d in range(2):
            mout_ref[sg, d:d + 1, :] = ms[d]
        return carry

    lax.fori_loop(0, nseg, scan_m, 0)

    def increments(g, carry):
        for u in range(MLSTM_GROUP):
            n = g * MLSTM_GROUP + u
            r = pl.multiple_of(n * L, L)
            k = k_ref[pl.ds(r, L), :].astype(F32)
            v1 = jnp.concatenate([v_ref[pl.ds(r, L), :].astype(BF16), ones], axis=1)
            gcol = gh_ref[pl.ds(r, L), :]
            for d in range(2):
                w_t = jnp.exp(sc_ref[d, 0, pl.ds(n, 1), 0:1] - gcol[:, 2 + d:3 + d] + gcol[:, d:d + 1]
                              - sc_ref[d, 3, pl.ds(n, 1), 0:1])
                kw = k * (w_t * scale)
                kv_ref[d, n] = _dot_tn(kw.astype(BF16), v1)
                ksum_ref[d, pl.ds(n, 1), :] = jnp.sum(kw, axis=0, keepdims=True)
        return carry

    lax.fori_loop(0, nct // MLSTM_GROUP, increments, 0)

    def scan_state(sg, carry):
        if has_init:
            nrows = tuple(n0_ref[d, pl.ds(head, 1), :] for d in range(2))
            for d in range(2):
                ce_ref[d, :, 0:hd] = c0_ref[d]
                ce_ref[d, :, hd:2 * hd] = jnp.broadcast_to(nrows[d], (hd, hd)).T
        else:
            nrows = (jnp.zeros((1, hd), F32),) * 2
            ce_ref[...] = jnp.zeros((2, hd, 2 * hd), F32)

        def step(c, nrows):
            out = []
            for d in range(2):
                n = chunk_of(sg, c, d)
                ce = ce_ref[d]
                cp_ref[d, n] = ce.astype(BF16)
                w_s = jnp.exp(sc_ref[d, 0, pl.ds(n, 1), 0:1] + sc_ref[d, 2, pl.ds(n, 1), 0:1]
                              - sc_ref[d, 3, pl.ds(n, 1), 0:1])
                ce_ref[d] = w_s * ce + kv_ref[d, n]
                out.append(w_s * nrows[d] + ksum_ref[d, pl.ds(n, 1), :])
            return tuple(out)

        nrows = lax.fori_loop(0, segc, step, nrows)
        for d in range(2):
            cout_ref[sg, d] = ce_ref[d, :, 0:hd]
            nout_ref[sg, d:d + 1, :] = nrows[d]
        return carry

    lax.fori_loop(0, nseg, scan_state, 0)

    def outputs(g, carry):
        for u in range(MLSTM_GROUP):
            n = g * MLSTM_GROUP + u
            r = pl.multiple_of(n * L, L)
            qb = q_ref[pl.ds(r, L), :].astype(BF16)
            v1 = jnp.concatenate([v_ref[pl.ds(r, L), :].astype(BF16), ones], axis=1)
            qk = _dot_nt(qb, (k_ref[pl.ds(r, L), :].astype(F32) * scale).astype(BF16))
            gcol = gh_ref[pl.ds(r, L), :]
            for d in range(2):
                bcol = jnp.broadcast_to(gcol[:, 2 + d:3 + d], (L, L))
                m_prev = sc_ref[d, 2, pl.ds(n, 1), 0:1]
                logd = jnp.where(masks[d], bcol - rows_ref[2 + d, pl.ds(n, 1), :] + rows_ref[d, pl.ds(n, 1), :],
                                 NEG_BIG)
                m_i = jnp.maximum(bcol + m_prev, jnp.broadcast_to(jnp.max(logd, axis=1, keepdims=True), (L, L)))
                s = qk * jnp.exp(logd - m_i)
                w_prev = jnp.exp(bcol + m_prev - m_i)
                qc = _dot(qb, cp_ref[d, n])
                sv = _dot(s.astype(BF16), v1)
                num = w_prev * qc[:, 0:hd] + sv[:, 0:hd]
                den = w_prev * qc[:, hd:2 * hd] + sv[:, hd:2 * hd]
                h_refs[d][pl.ds(r, L), :] = num / jnp.maximum(jnp.abs(den), jnp.exp(-m_i))
        return carry

    lax.fori_loop(0, nct // MLSTM_GROUP, outputs, 0)
    y = _head_out(hf_ref[...] + hb_ref[...], hng_ref[...], _sigmoid(o_ref[...].astype(F32)))
    y_ref[...] = y.astype(y_ref.dtype)


def _ret_kernel(nseg, has_init, use_rope, *refs):
    L = CHUNK_C
    refs = list(refs)
    q_ref, k_ref, v_ref, g_ref, lg_ref, hng_ref = refs[:6]
    pos = 6
    if use_rope:
        cos_ref, sin_ref = refs[pos:pos + 2]
        pos += 2
    if has_init:
        s0_ref = refs[pos]
        pos += 2
    y_ref, sout_ref, o_ref, qs_ref, ks_ref, kv_ref, sp_ref = refs[pos:]
    rb = q_ref.shape[0]
    nct = rb // L
    segc = nct // nseg
    hd = HEAD_DIM

    q = q_ref[...].astype(F32) * HEAD_DIM ** -0.5
    k = k_ref[...].astype(F32)
    if use_rope:
        lane = lax.broadcasted_iota(jnp.int32, (rb, HEAD_DIM), 1)
        cos = cos_ref[...]
        sin = sin_ref[...]
        half = HEAD_DIM // 2

        def rope(x):
            rot = pltpu.roll(x, half, 1)
            return x * cos + jnp.where(lane < half, -rot, rot) * sin
        q = rope(q)
        k = rope(k)
    qs_ref[...] = q
    ks_ref[...] = k

    lg_f = lg_ref[0:1, 0:1]
    lg_b = lg_ref[1:2, 0:1]
    i, j = _tri_masks(L)
    dij = (i - j).astype(F32)
    decay_f = jnp.where(i >= j, jnp.exp(lg_f * jnp.maximum(dij, 0.0)), 0.0)
    decay_b = jnp.where(i <= j, jnp.exp(lg_b * jnp.maximum(-dij, 0.0)), 0.0)
    pos_c = lax.broadcasted_iota(jnp.int32, (L, 1), 0).astype(F32)
    qdec_f = jnp.exp(lg_f * (pos_c + 1.0))
    kend_f = jnp.exp(lg_f * (L - 1.0 - pos_c))
    qdec_b = jnp.exp(lg_b * (L - pos_c))
    kend_b = jnp.exp(lg_b * pos_c)
    cdec_f = jnp.exp(lg_f * L)
    cdec_b = jnp.exp(lg_b * L)

    decay = decay_f + decay_b

    def increments(g, carry):
        for u in range(RET_GROUP):
            n = g * RET_GROUP + u
            r = pl.multiple_of(n * L, L)
            kc = ks_ref[pl.ds(r, L), :]
            ke = jnp.concatenate([kc * kend_f, kc * kend_b], axis=1).astype(BF16)
            kv_ref[n] = _dot_tn(ke, v_ref[pl.ds(r, L), :].astype(BF16))
        return carry

    lax.fori_loop(0, nct // RET_GROUP, increments, 0)

    def scan_state(sg, carry):
        if has_init:
            st = (s0_ref[0], s0_ref[1])
        else:
            st = (jnp.zeros((hd, hd), F32),) * 2

        def step(c, st):
            nf = sg * segc + c
            nb = sg * segc + (segc - 1 - c)
            sp_ref[nf, 0:hd, :] = st[0].astype(BF16)
            sp_ref[nb, hd:2 * hd, :] = st[1].astype(BF16)
            return (cdec_f * st[0] + kv_ref[nf, 0:hd, :], cdec_b * st[1] + kv_ref[nb, hd:2 * hd, :])

        st = lax.fori_loop(0, segc, step, st)
        sout_ref[sg, 0] = st[0]
        sout_ref[sg, 1] = st[1]
        return carry

    lax.fori_loop(0, nseg, scan_state, 0)

    def outputs(g, carry):
        for u in range(RET_GROUP):
            n = g * RET_GROUP + u
            r = pl.multiple_of(n * L, L)
            qc = qs_ref[pl.ds(r, L), :]
            qk = _dot_nt(qc.astype(BF16), ks_ref[pl.ds(r, L), :].astype(BF16))
            inner = _dot((qk * decay).astype(BF16), v_ref[pl.ds(r, L), :].astype(BF16))
            qd = jnp.concatenate([qc * qdec_f, qc * qdec_b], axis=1).astype(BF16)
            o_ref[pl.ds(r, L), :] = inner + _dot(qd, sp_ref[n])
        return carry

    lax.fori_loop(0, nct // RET_GROUP, outputs, 0)
    gate = g_ref[...].astype(F32)
    y = _head_out(o_ref[...], hng_ref[...], gate * _sigmoid(gate))
    y_ref[...] = y.astype(y_ref.dtype)


def _retention(p, lgrows, hng, rope, y_prev, s0, *, blk0, nblk, nseg, layer, col0, rb=ROW_GROUP):
    m = p.shape[0]
    assert (rb // CHUNK_C) % RET_GROUP == 0
    colspec = lambda c0: pl.BlockSpec((rb, HEAD_DIM), lambda b, h: (blk0 + b, c0 + h))
    in_specs = [colspec(col0), colspec(col0 + 8), colspec(col0 + 16), colspec(col0 + 24),
                pl.BlockSpec((None, 8, HEAD_DIM), lambda b, h: (h, 0, 0)),
                pl.BlockSpec((1, HEAD_DIM), lambda b, h: (0, 0))]
    args = [p, p, p, p, lgrows, hng]
    if rope is not None:
        in_specs += [pl.BlockSpec((rb, HEAD_DIM), lambda b, h: (0, 0))] * 2
        args += list(rope)
    aliases = {}
    if s0 is not None:
        in_specs += [pl.BlockSpec((None, None, 2, None, HEAD_DIM, HEAD_DIM), lambda b, h: (b, layer, 0, h, 0, 0)),
                     pl.BlockSpec(memory_space=pl.ANY)]
        args += [s0, y_prev]
        aliases = {len(args) - 1: 0}
    nsq = nblk * nseg
    return pl.pallas_call(
        functools.partial(_ret_kernel, nseg, s0 is not None, rope is not None),
        grid=(nblk, N_HEADS),
        in_specs=in_specs,
        out_specs=[pl.BlockSpec((rb, HEAD_DIM), lambda b, h: (blk0 + b, h)),
                   pl.BlockSpec((nseg, 2, None, HEAD_DIM, HEAD_DIM), lambda b, h: (b, 0, h, 0, 0))],
        out_shape=[jax.ShapeDtypeStruct((m, W_MIX), BF16),
                   jax.ShapeDtypeStruct((nsq, 2, N_HEADS, HEAD_DIM, HEAD_DIM), F32)],
        scratch_shapes=[pltpu.VMEM((rb, HEAD_DIM), F32), pltpu.VMEM((rb, HEAD_DIM), F32),
                        pltpu.VMEM((rb, HEAD_DIM), F32),
                        pltpu.VMEM((rb // CHUNK_C, 2 * HEAD_DIM, HEAD_DIM), F32),
                        pltpu.VMEM((rb // CHUNK_C, 2 * HEAD_DIM, HEAD_DIM), BF16)],
        input_output_aliases=aliases,
        compiler_params=_params(("arbitrary", "arbitrary")),
        name="retention",
    )(*args)


def _inv_unit_triangular(mats, eye, i, j):
    n = eye.shape[0]
    xs = [eye - jnp.where((i >> 1) == (j >> 1), a, 0.0) for a in mats]
    s, sh = 2, 1
    while s < n:
        join = ((i >> (sh + 1)) == (j >> (sh + 1))) & ((i >> sh) != (j >> sh))
        xb = [x.astype(BF16) for x in xs]
        mt = [_dot(jnp.where(join, a, 0.0).astype(BF16), b).astype(BF16) for a, b in zip(mats, xb)]
        xs = [x - _dot(b, t) for x, b, t in zip(xs, xb, mt)]
        s, sh = 2 * s, sh + 1
    rs = [(eye - x - _dot_x3(a, x)).astype(BF16) for a, x in zip(mats, xs)]
    return [x + _dot(x.astype(BF16), r) for x, r in zip(xs, rs)]


def _delta_kernel(nseg, has_init, *refs):
    L = CHUNK_B
    refs = list(refs)
    q_ref, k_ref, v_ref, z_ref, cwq_ref, cwk_ref, cwv_ref, gc_ref, rows_ref, hng_ref = refs[:10]
    pos = 10
    if has_init:
        s0_ref = refs[pos]
        pos += 2
    (y_ref, sout_ref, qs_ref, ks_ref, vs_ref, gh_ref, u0f_ref, u0b_ref, wkf_ref, wkb_ref,
     qkf_ref, qkb_ref, qdf_ref, qdb_ref, ketf_ref, ketb_ref, sf_ref, sb_ref) = refs[pos:]
    of_ref, ob_ref = qs_ref, vs_ref
    rb = q_ref.shape[0]
    tseg = rb // nseg
    segc = tseg // L
    nchunks = rb // L

    row = lax.broadcasted_iota(jnp.int32, (rb, HEAD_DIM), 0)
    rowmod = row % tseg
    first = rowmod == 0
    last = rowmod == tseg - 1

    def conv_silu(x_ref, w_ref):
        x = x_ref[...].astype(F32)
        prev = jnp.where(first, 0.0, pltpu.roll(x, 1, 0))
        nxt = jnp.where(last, 0.0, pltpu.roll(x, rb - 1, 0))
        y = w_ref[0:1, :] * prev + w_ref[1:2, :] * x + w_ref[2:3, :] * nxt
        return y * _sigmoid(y)

    def l2n(x):
        return x * lax.rsqrt(jnp.sum(x * x, axis=-1, keepdims=True) + EPS)

    qs_ref[...] = l2n(conv_silu(q_ref, cwq_ref)) * HEAD_DIM ** -0.5
    ks_ref[...] = l2n(conv_silu(k_ref, cwk_ref))
    vs_ref[...] = conv_silu(v_ref, cwv_ref)
    gh_ref[...] = _select_cols(gc_ref[...], 32)

    i, j = _tri_masks(L)
    eye = (i == j).astype(F32)
    masks = ((i >= j, i > j), (i <= j, i < j))

    dir_refs = ((u0f_ref, wkf_ref, qkf_ref, qdf_ref, ketf_ref), (u0b_ref, wkb_ref, qkb_ref, qdb_ref, ketb_ref))

    def g_last(rw, d):
        return rw[2:3, L - 1:L] if d == 0 else rw[3:4, 0:1]

    def prep(gi, carry):
        chains = []
        for u in range(PREP_GROUP):
            c = gi * PREP_GROUP + u
            r = pl.multiple_of(c * L, L)
            qc = qs_ref[pl.ds(r, L), :]
            kc = ks_ref[pl.ds(r, L), :]
            kb = kc.astype(BF16)
            kk = _dot_nt(kb, kb)
            qk = _dot_nt(qc.astype(BF16), kb)
            gcol = gh_ref[pl.ds(r, L), :]
            rw = rows_ref[c]
            for d in range(2):
                incl, strict = masks[d]
                g_c = gcol[:, 2 + d:3 + d]
                decay = jnp.where(incl, jnp.exp(jnp.where(incl, g_c - rw[2 + d:3 + d, :], 0.0)), 0.0)
                dir_refs[d][2][pl.ds(r, L), :] = (qk * decay).astype(BF16)
                dir_refs[d][3][pl.ds(r, L), :] = (qc * jnp.exp(g_c)).astype(BF16)
                dir_refs[d][4][c] = (kc * jnp.exp(g_last(rw, d) - g_c)).T.astype(BF16)
                chains.append((r, d, jnp.where(strict, kk * decay, 0.0) * gcol[:, d:d + 1]))
        invs = _inv_unit_triangular([a for _, _, a in chains], eye, i, j)
        for (r, d, _), t in zip(chains, invs):
            gcol = gh_ref[pl.ds(r, L), :]
            beta_c = gcol[:, d:d + 1]
            bg = beta_c * jnp.exp(gcol[:, 2 + d:3 + d])
            tb = t.astype(BF16)
            dir_refs[d][0][pl.ds(r, L), :] = _dot(tb, (beta_c * vs_ref[pl.ds(r, L), :]).astype(BF16))
            dir_refs[d][1][pl.ds(r, L), :] = _dot(tb, (bg * ks_ref[pl.ds(r, L), :]).astype(BF16)).astype(BF16)
        return carry

    lax.fori_loop(0, nchunks // PREP_GROUP, prep, 0)

    def one(c, d, s_ref, o_ref):
        u0_ref, wk_ref, qkd_ref, qd_ref, ket_ref = dir_refs[d]
        r = pl.multiple_of(c * L, L)
        s_prev = s_ref[...]
        sb = s_prev.astype(BF16)
        ub = (u0_ref[pl.ds(r, L), :] - _dot(wk_ref[pl.ds(r, L), :], sb)).astype(BF16)
        o_ref[pl.ds(r, L), :] = _dot(qd_ref[pl.ds(r, L), :], sb) + _dot(qkd_ref[pl.ds(r, L), :], ub)
        s_ref[...] = jnp.exp(g_last(rows_ref[c], d)) * s_prev + _dot(ket_ref[c], ub)

    def segment(sg, carry):
        if has_init:
            sf_ref[...] = s0_ref[0]
            sb_ref[...] = s0_ref[1]
        else:
            sf_ref[...] = jnp.zeros((HEAD_DIM, HEAD_DIM), F32)
            sb_ref[...] = jnp.zeros((HEAD_DIM, HEAD_DIM), F32)

        def step(c, carry2):
            one(sg * segc + c, 0, sf_ref, of_ref)
            one(sg * segc + (segc - 1 - c), 1, sb_ref, ob_ref)
            return carry2

        lax.fori_loop(0, segc, step, 0)
        sout_ref[sg, 0] = sf_ref[...]
        sout_ref[sg, 1] = sb_ref[...]
        return carry

    lax.fori_loop(0, nseg, segment, 0)
    gate = z_ref[...].astype(F32)
    y = _head_out(of_ref[...] + ob_ref[...], hng_ref[...], gate * _sigmoid(gate))
    y_ref[...] = y.astype(y_ref.dtype)


def _delta(p, conv_w, gc, rows, hng, y_prev, s0, *, blk0, nblk, nseg, layer, col0, rb=ROW_GROUP):
    m = p.shape[0]
    cb = rb // CHUNK_B
    assert cb % PREP_GROUP == 0
    colspec = lambda c0: pl.BlockSpec((rb, HEAD_DIM), lambda b, h: (blk0 + b, c0 + h))
    cwspec = lambda c0: pl.BlockSpec((None, CONV_K, HEAD_DIM), lambda b, h: (layer, 0, c0 + h))
    in_specs = [colspec(col0), colspec(col0 + 8), colspec(col0 + 16), colspec(col0 + 24),
                cwspec(0), cwspec(8), cwspec(16),
                pl.BlockSpec((rb, GATE_LANES), lambda b, h: (blk0 + b, 0)),
                pl.BlockSpec((None, cb, 8, CHUNK_B), lambda b, h: (h, blk0 + b, 0, 0)),
                pl.BlockSpec((1, HEAD_DIM), lambda b, h: (0, 0))]
    args = [p, p, p, p, conv_w, conv_w, conv_w, gc, rows, hng]
    aliases = {}
    if s0 is not None:
        in_specs += [pl.BlockSpec((None, None, 2, None, HEAD_DIM, HEAD_DIM), lambda b, h: (b, layer, 0, h, 0, 0)),
                     pl.BlockSpec(memory_space=pl.ANY)]
        args += [s0, y_prev]
        aliases = {len(args) - 1: 0}
    nsq = nblk * nseg
    big = lambda dt: pltpu.VMEM((rb, HEAD_DIM), dt)
    return pl.pallas_call(
        functools.partial(_delta_kernel, nseg, s0 is not None),
        grid=(nblk, N_HEADS),
        in_specs=in_specs,
        out_specs=[pl.BlockSpec((rb, HEAD_DIM), lambda b, h: (blk0 + b, h)),
                   pl.BlockSpec((nseg, 2, None, HEAD_DIM, HEAD_DIM), lambda b, h: (b, 0, h, 0, 0))],
        out_shape=[jax.ShapeDtypeStruct((m, W_MIX), BF16),
                   jax.ShapeDtypeStruct((nsq, 2, N_HEADS, HEAD_DIM, HEAD_DIM), F32)],
        scratch_shapes=[big(F32), big(F32), big(F32), pltpu.VMEM((rb, GATE_LANES), F32),
                        big(F32), big(F32), big(BF16), big(BF16),
                        pltpu.VMEM((rb, CHUNK_B), BF16), pltpu.VMEM((rb, CHUNK_B), BF16),
                        big(BF16), big(BF16),
                        pltpu.VMEM((cb, HEAD_DIM, CHUNK_B), BF16), pltpu.VMEM((cb, HEAD_DIM, CHUNK_B), BF16),
                        pltpu.VMEM((HEAD_DIM, HEAD_DIM), F32), pltpu.VMEM((HEAD_DIM, HEAD_DIM), F32)],
        input_output_aliases=aliases,
        compiler_params=_params(("arbitrary", "arbitrary")),
        name="deltanet",
    )(*args)


def _state_specs(nseg):
    specs = [pl.BlockSpec((nseg, 2, None, HEAD_DIM, HEAD_DIM), lambda b, h: (b, 0, h, 0, 0)),
             pl.BlockSpec((None, nseg, 2, HEAD_DIM), lambda b, h: (h, b, 0, 0)),
             pl.BlockSpec((None, nseg, 2, HEAD_DIM), lambda b, h: (h, b, 0, 0))]
    return specs


def _mlstm(p, gc, rows, hng, y_prev, init, *, blk0, nblk, nseg, layer, rb=ROW_GROUP):
    m = p.shape[0]
    cb = rb // CHUNK_A
    assert cb % MLSTM_GROUP == 0
    colspec = lambda c0: pl.BlockSpec((rb, HEAD_DIM), lambda b, h: (blk0 + b, c0 + h))
    in_specs = [colspec(0), colspec(8), colspec(16), colspec(24),
                pl.BlockSpec((rb, GATE_LANES), lambda b, h: (blk0 + b, 0)),
                pl.BlockSpec((None, 8, cb, CHUNK_A), lambda b, h: (h, 0, blk0 + b, 0)),
                pl.BlockSpec((1, HEAD_DIM), lambda b, h: (0, 0))]
    args = [p, p, p, p, gc, rows, hng]
    if init is not None:
        c0, n0, m0 = init
        in_specs += [pl.BlockSpec((None, None, 2, None, HEAD_DIM, HEAD_DIM), lambda b, h: (b, layer, 0, h, 0, 0)),
                     pl.BlockSpec((None, None, 2, N_HEADS, HEAD_DIM), lambda b, h: (b, layer, 0, 0, 0)),
                     pl.BlockSpec((None, None, 2, N_HEADS, HEAD_DIM), lambda b, h: (b, layer, 0, 0, 0)),
                     pl.BlockSpec(memory_space=pl.ANY)]
        args += [c0, n0, m0, y_prev]
        aliases = {len(args) - 1: 0}
    else:
        aliases = {}
    nsq = nblk * nseg
    out_shape = [jax.ShapeDtypeStruct((m, W_MIX), BF16),
                 jax.ShapeDtypeStruct((nsq, 2, N_HEADS, HEAD_DIM, HEAD_DIM), F32),
                 jax.ShapeDtypeStruct((N_HEADS, nsq, 2, HEAD_DIM), F32),
                 jax.ShapeDtypeStruct((N_HEADS, nsq, 2, HEAD_DIM), F32)]
    out_specs = [pl.BlockSpec((rb, HEAD_DIM), lambda b, h: (blk0 + b, h))] + _state_specs(nseg)
    return pl.pallas_call(
        functools.partial(_mlstm_kernel, nseg, init is not None),
        grid=(nblk, N_HEADS),
        in_specs=in_specs, out_specs=out_specs, out_shape=out_shape,
        scratch_shapes=[pltpu.VMEM((rb, HEAD_DIM), F32), pltpu.VMEM((rb, HEAD_DIM), F32),
                        pltpu.VMEM((rb, GATE_LANES), F32),
                        pltpu.VMEM((2, 4, cb, HEAD_DIM), F32),
                        pltpu.VMEM((2, cb, HEAD_DIM, 2 * HEAD_DIM), F32), pltpu.VMEM((2, cb, HEAD_DIM), F32),
                        pltpu.VMEM((2, cb, HEAD_DIM, 2 * HEAD_DIM), BF16),
                        pltpu.VMEM((2, HEAD_DIM, 2 * HEAD_DIM), F32)],
        input_output_aliases=aliases,
        compiler_params=_params(("arbitrary", "arbitrary")),
        name="mlstm",
    )(*args)


_MAIN_COLS = ((0, 4 * W_MIX), (4 * W_MIX + 32, 8 * W_MIX + 32), (8 * W_MIX + 64, 12 * W_MIX + 64 + 3 * D_MODEL))
_GATE_COLS = ((4 * W_MIX, 4 * W_MIX + 32), (8 * W_MIX + 32, 8 * W_MIX + 64))
COL_A, COL_B, COL_C = 0, 32, 64


def _take_cols(w, ranges):
    return jnp.concatenate([w[..., a:b] for a, b in ranges], axis=-1)


def _rope_tables(t):
    pos = jnp.arange(t)
    pos_r = (pos // GRID_W).astype(F32)
    pos_c = (pos % GRID_W).astype(F32)
    nf = HEAD_DIM // 4
    freqs = ROPE_BASE ** (-jnp.arange(nf, dtype=F32) / nf)
    ang = jnp.concatenate([pos_r[:, None] * freqs, pos_c[:, None] * freqs], axis=-1)
    ang = jnp.concatenate([ang, ang], axis=-1)
    return jnp.cos(ang), jnp.sin(ang)


def kernel(x_prompt, x_sample, state_mlstm_C, state_mlstm_n, state_mlstm_m, state_delta_S, state_ret_S,
           c, c_ctx, norm_g, final_norm_g, w_ada, b_ada, w_in, b_in, mlstm_f_bias, conv_w, delta_A_log,
           delta_dt_bias, ret_log_gamma, head_norm_g, w_br, w_out, ffn_w13, ffn_w2):
    bp, sp, d = x_prompt.shape
    bs, ss, _ = x_sample.shape
    assert d == D_MODEL and bp * sp == ROW_GROUP and ss == ROW_GROUP
    n_prompt_blk, n_sample_blk = 1, bs
    x = jnp.concatenate([x_prompt.reshape(-1, d), x_sample.reshape(-1, d)], axis=0)

    cv8 = jnp.zeros((8, d), F32).at[0].set(c_ctx).at[1:1 + bs].set(c)
    mod = _ada(cv8, w_ada, b_ada)[:, :1 + bs].reshape(DEPTH, 1 + bs, N_MOD, 1, d)
    rope = _rope_tables(ss)
    m0_pad = jnp.broadcast_to(state_mlstm_m[..., None], state_mlstm_m.shape + (HEAD_DIM,))

    b_main = _take_cols(b_in, _MAIN_COLS)
    pad = GATE_LANES - 64
    w_gate = jnp.pad(_take_cols(w_in, _GATE_COLS), ((0, 0), (0, 0), (0, pad))).astype(BF16)
    b_gate = jnp.pad(_take_cols(b_in, _GATE_COLS), ((0, 0), (0, pad)))

    new_c, new_n, new_m, new_sd, new_sr = [], [], [], [], []
    for l in range(DEPTH):
        md = lambda i: mod[l, :, i]
        h = _normmod(x, norm_g[l, 0], md(0), md(1))
        x = _mm_res(_mm_swiglu(h, ffn_w13, (l, 0)), ffn_w2, (l, 0), x, md(2), 0.5, 1024, 512)

        h = _normmod(x, norm_g[l, 1], md(3), md(4))
        p = _in_proj(h, w_in, b_main, l)
        graw = _mm_bias(h, w_gate, b_gate, l, GATE_LANES)
        pv = jnp.zeros((8, GATE_LANES), F32)
        pv = pv.at[0, 16:32].set(mlstm_f_bias[l].reshape(-1))
        pv = pv.at[1, 48:64].set(delta_dt_bias[l].reshape(-1))
        pv = pv.at[2, 48:64].set(delta_A_log[l].reshape(-1))
        gc, grt = _gateprep(graw, pv)
        rows_a = _gate_rows(grt, 0, CHUNK_A, chunk_major=False)
        rows_b = _gate_rows(grt, 32, CHUNK_B)
        hng = head_norm_g[l].reshape(3, 1, HEAD_DIM)
        lgrows = jnp.zeros((N_HEADS, 8, HEAD_DIM), F32).at[:, 0:2, :].set(
            jnp.broadcast_to(ret_log_gamma[l].T[:, :, None], (N_HEADS, 2, HEAD_DIM)))
        prompt = dict(blk0=0, nblk=n_prompt_blk, nseg=bp, layer=l)
        sample = dict(blk0=n_prompt_blk, nblk=n_sample_blk, nseg=1, layer=l)

        ya, c_new, n_new, m_new = _mlstm(p, gc, rows_a, hng[0], None, None, **prompt)
        ya = _mlstm(p, gc, rows_a, hng[0], ya, (state_mlstm_C, state_mlstm_n, m0_pad), **sample)[0]
        yb, sd_new = _delta(p, conv_w, gc, rows_b, hng[1], None, None, col0=COL_B, **prompt)
        yb = _delta(p, conv_w, gc, rows_b, hng[1], yb, state_delta_S, col0=COL_B, **sample)[0]
        yc, sr_new = _retention(p, lgrows, hng[2], None, None, None, col0=COL_C, **prompt)
        yc = _retention(p, lgrows, hng[2], rope, yc, state_ret_S, col0=COL_C, **sample)[0]
        new_c.append(c_new)
        new_n.append(jnp.transpose(n_new, (1, 2, 0, 3)))
        new_m.append(jnp.transpose(m_new[..., 0], (1, 2, 0)))
        new_sd.append(sd_new)
        new_sr.append(sr_new)

        x = _mm_res(_merge(ya, yb, yc, w_br, l, p), w_out, (l,), x, md(5), 1.0, 1024, 1024)
        h = _normmod(x, norm_g[l, 2], md(6), md(7))
        x = _mm_res(_mm_swiglu(h, ffn_w13, (l, 1)), ffn_w2, (l, 1), x, md(8), 0.5, 1024, 512)

    y_prompt = _final_norm(x, final_norm_g, 0, bp * sp).reshape(bp, sp, d)
    y_sample = _final_norm(x, final_norm_g, bp * sp, bs * ss).reshape(bs, ss, d)
    stack = lambda xs: jnp.stack(xs, axis=1)
    return (y_prompt, y_sample, stack(new_c), stack(new_n), stack(new_m), stack(new_sd), stack(new_sr))
```

```python
import functools

import jax
import jax.numpy as jnp
import numpy as np
from jax import lax
from jax.experimental import pallas as pl
from jax.experimental.pallas import tpu as pltpu

F32 = jnp.float32
BF16 = jnp.bfloat16

D_MODEL = 2048
DEPTH = 2
N_HEADS = 8
HEAD_DIM = 128
W_MIX = N_HEADS * HEAD_DIM
D_FF = 4096
N_MOD = 9
CONV_K = 3
GRID_W = 64
ROPE_BASE = 10000.0
EPS = 1e-6
ROW_GROUP = 4096
N_MAIN = 12 * W_MIX + 3 * D_MODEL
GATE_LANES = 128
CHUNK_A = 128
CHUNK_B = 128
RET_GROUP = 4
MLSTM_GROUP = 2
PREP_GROUP = 8
CHUNK_C = 128
NEG_BIG = -1e30
VMEM_LIMIT = 56 * 1024 * 1024


def _params(sem):
    return pltpu.CompilerParams(dimension_semantics=sem, vmem_limit_bytes=VMEM_LIMIT)


def _dot(a, b):
    return jnp.dot(a, b, preferred_element_type=F32)


def _dot_nt(a, b):
    return lax.dot_general(a, b, (((1,), (1,)), ((), ())), preferred_element_type=F32)


def _dot_tn(a, b):
    return lax.dot_general(a, b, (((0,), (0,)), ((), ())), preferred_element_type=F32)


def _split3(x):
    hi = x.astype(BF16)
    r1 = x - hi.astype(F32)
    mid = r1.astype(BF16)
    lo = (r1 - mid.astype(F32)).astype(BF16)
    return hi, mid, lo


def _dot_exact_rhs01(x, sel):
    hi, mid, lo = _split3(x)
    return _dot(hi, sel) + _dot(mid, sel) + _dot(lo, sel)


def _dot_x3(a, b):
    ah = a.astype(BF16)
    al = (a - ah.astype(F32)).astype(BF16)
    bh = b.astype(BF16)
    bl = (b - bh.astype(F32)).astype(BF16)
    return _dot(ah, bh) + _dot(ah, bl) + _dot(al, bh)


def _sigmoid(x):
    return 0.5 * jnp.tanh(0.5 * x) + 0.5


def _softplus(x):
    return jnp.maximum(x, 0.0) + jnp.log(1.0 + jnp.exp(-jnp.abs(x)))


def _ada_kernel(cv_ref, w_ref, b_ref, o_ref):
    cv = cv_ref[...]
    s = (cv * _sigmoid(cv)).astype(BF16)
    o_ref[...] = _dot(s, w_ref[...].astype(BF16)) + b_ref[...]


def _ada(cv8, w_ada, b_ada):
    n = w_ada.shape[-1]
    bn = 1024
    return pl.pallas_call(
        _ada_kernel,
        grid=(DEPTH, n // bn),
        in_specs=[pl.BlockSpec((8, D_MODEL), lambda l, j: (0, 0)),
                  pl.BlockSpec((None, D_MODEL, bn), lambda l, j: (l, 0, j)),
                  pl.BlockSpec((None, 1, bn), lambda l, j: (l, 0, j))],
        out_specs=pl.BlockSpec((None, 8, bn), lambda l, j: (l, 0, j)),
        out_shape=jax.ShapeDtypeStruct((DEPTH, 8, n), F32),
        compiler_params=_params(("arbitrary", "arbitrary")),
        name="adaln",
    )(cv8, w_ada, b_ada.reshape(DEPTH, 1, n))


def _normmod_kernel(x_ref, g_ref, sh_ref, sc_ref, o_ref):
    x = x_ref[...]
    y = x * lax.rsqrt(jnp.mean(x * x, axis=-1, keepdims=True) + EPS) * g_ref[...]
    o_ref[...] = (y * (1.0 + sc_ref[...]) + sh_ref[...]).astype(o_ref.dtype)


def _normmod(x, g, shift, scale):
    m = x.shape[0]
    bm = 512
    grp = ROW_GROUP // bm
    return pl.pallas_call(
        _normmod_kernel,
        grid=(m // bm,),
        in_specs=[pl.BlockSpec((bm, D_MODEL), lambda i: (i, 0)),
                  pl.BlockSpec((1, D_MODEL), lambda i: (0, 0)),
                  pl.BlockSpec((None, 1, D_MODEL), lambda i: (i // grp, 0, 0)),
                  pl.BlockSpec((None, 1, D_MODEL), lambda i: (i // grp, 0, 0))],
        out_specs=pl.BlockSpec((bm, D_MODEL), lambda i: (i, 0)),
        out_shape=jax.ShapeDtypeStruct((m, D_MODEL), BF16),
        compiler_params=_params(("arbitrary",)),
        name="normmod",
    )(x, g.reshape(1, D_MODEL), shift, scale)


def _rmsnorm_kernel(x_ref, g_ref, o_ref):
    x = x_ref[...]
    o_ref[...] = x * lax.rsqrt(jnp.mean(x * x, axis=-1, keepdims=True) + EPS) * g_ref[...]


def _final_norm(x, g, row0, rows):
    bm = 512
    off = row0 // bm
    return pl.pallas_call(
        _rmsnorm_kernel,
        grid=(rows // bm,),
        in_specs=[pl.BlockSpec((bm, D_MODEL), lambda i: (i + off, 0)),
                  pl.BlockSpec((1, D_MODEL), lambda i: (0, 0))],
        out_specs=pl.BlockSpec((bm, D_MODEL), lambda i: (i, 0)),
        out_shape=jax.ShapeDtypeStruct((rows, D_MODEL), F32),
        compiler_params=_params(("arbitrary",)),
        name="final_norm",
    )(x, g.reshape(1, D_MODEL))


def _swiglu_kernel(h_ref, wa_ref, wb_ref, o_ref, was, wbs):
    @pl.when(pl.program_id(1) == 0)
    def _():
        was[...] = wa_ref[...].astype(BF16)
        wbs[...] = wb_ref[...].astype(BF16)
    h = h_ref[...]
    a = _dot(h, was[...])
    b = _dot(h, wbs[...])
    o_ref[...] = (a * _sigmoid(a) * b).astype(o_ref.dtype)


def _wspec(widx, k, bn, col=lambda j: j):
    lead = tuple(widx)
    return pl.BlockSpec((None,) * len(lead) + (k, bn), lambda j, i: lead + (0, col(j)))


def _mm_swiglu(h, w13, widx):
    m, k = h.shape
    f = w13.shape[-1] // 2
    bm, bn = 1024, 512
    nb = f // bn
    return pl.pallas_call(
        _swiglu_kernel,
        grid=(nb, m // bm),
        in_specs=[pl.BlockSpec((bm, k), lambda j, i: (i, 0)),
                  _wspec(widx, k, bn),
                  _wspec(widx, k, bn, lambda j: j + nb)],
        out_specs=pl.BlockSpec((bm, bn), lambda j, i: (i, j)),
        out_shape=jax.ShapeDtypeStruct((m, f), BF16),
        scratch_shapes=[pltpu.VMEM((k, bn), BF16), pltpu.VMEM((k, bn), BF16)],
        compiler_params=_params(("arbitrary", "arbitrary")),
        name="ffn_up",
    )(h, w13, w13)


def _mm_res_kernel(coef, a_ref, w_ref, x_ref, s_ref, o_ref, ws):
    @pl.when(pl.program_id(1) == 0)
    def _():
        ws[...] = w_ref[...].astype(BF16)
    acc = _dot(a_ref[...], ws[...])
    o_ref[...] = x_ref[...] + (coef * s_ref[...]) * acc


def _mm_res(a, w, widx, x, s, coef, bm, bn):
    m, k = a.shape
    n = w.shape[-1]
    grp = ROW_GROUP // bm
    return pl.pallas_call(
        functools.partial(_mm_res_kernel, coef),
        grid=(n // bn, m // bm),
        in_specs=[pl.BlockSpec((bm, k), lambda j, i: (i, 0)),
                  _wspec(widx, k, bn),
                  pl.BlockSpec((bm, bn), lambda j, i: (i, j)),
                  pl.BlockSpec((None, 1, bn), lambda j, i: (i // grp, 0, j))],
        out_specs=pl.BlockSpec((bm, bn), lambda j, i: (i, j)),
        out_shape=jax.ShapeDtypeStruct((m, n), F32),
        scratch_shapes=[pltpu.VMEM((k, bn), BF16)],
        compiler_params=_params(("arbitrary", "arbitrary")),
        name="mm_residual",
    )(a, w, x, s)


IN_BN = 1024
_IN_SEGMENTS = ((0, 4, 0), (4, 8, 32), (8, 18, 64))


def _in_proj_kernel(a_ref, wa_ref, wb_ref, b_ref, o_ref, ws):
    j = pl.program_id(0)
    first_row_tile = pl.program_id(1) == 0
    rows = 256
    for lo, hi, shift in _IN_SEGMENTS:
        @pl.when(first_row_tile & (j >= lo) & (j < hi))
        def _(shift=shift):
            for r in range(0, IN_BN - shift, rows):
                n = min(rows, IN_BN - shift - r)
                ws[r:r + n, :] = wa_ref[shift + r:shift + r + n, :].astype(BF16)
            if shift:
                ws[IN_BN - shift:IN_BN, :] = wb_ref[0:shift, :].astype(BF16)
    o_ref[...] = (_dot_nt(a_ref[...], ws[...]) + b_ref[...]).astype(o_ref.dtype)


def _in_proj(a, w_t, b_main, layer):
    m, k = a.shape
    bm, bn = 1024, IN_BN
    assert N_MAIN // bn == _IN_SEGMENTS[-1][1]
    return pl.pallas_call(
        _in_proj_kernel,
        grid=(N_MAIN // bn, m // bm),
        in_specs=[pl.BlockSpec((bm, k), lambda j, i: (i, 0)),
                  pl.BlockSpec((None, bn, k), lambda j, i: (layer, j, 0)),
                  pl.BlockSpec((None, 128, k), lambda j, i: (layer, (j + 1) * (bn // 128), 0)),
                  pl.BlockSpec((None, 1, bn), lambda j, i: (layer, 0, j))],
        out_specs=pl.BlockSpec((bm, bn), lambda j, i: (i, j)),
        out_shape=jax.ShapeDtypeStruct((m, N_MAIN), BF16),
        scratch_shapes=[pltpu.VMEM((bn, k), BF16)],
        compiler_params=_params(("arbitrary", "arbitrary")),
        name="in_proj",
    )(a, w_t, w_t, b_main.reshape(DEPTH, 1, N_MAIN))


def _gate_proj_kernel(a_ref, w1_ref, w2_ref, b_ref, o_ref, ws):
    @pl.when(pl.program_id(0) == 0)
    def _():
        ws[0:32, :] = w1_ref[0:32, :].astype(BF16)
        ws[32:64, :] = w2_ref[32:64, :].astype(BF16)
        ws[64:GATE_LANES, :] = jnp.zeros((GATE_LANES - 64, ws.shape[1]), BF16)
    o_ref[...] = _dot_nt(a_ref[...], ws[...]) + b_ref[...]


def _gate_proj(a, w_t, b_gate, layer):
    m, k = a.shape
    bm = 1024
    return pl.pallas_call(
        _gate_proj_kernel,
        grid=(m // bm,),
        in_specs=[pl.BlockSpec((bm, k), lambda i: (i, 0)),
                  pl.BlockSpec((None, 128, k), lambda i: (layer, 4 * W_MIX // 128, 0)),
                  pl.BlockSpec((None, 128, k), lambda i: (layer, 8 * W_MIX // 128, 0)),
                  pl.BlockSpec((None, 1, GATE_LANES), lambda i: (layer, 0, 0))],
        out_specs=pl.BlockSpec((bm, GATE_LANES), lambda i: (i, 0)),
        out_shape=jax.ShapeDtypeStruct((m, GATE_LANES), F32),
        scratch_shapes=[pltpu.VMEM((GATE_LANES, k), BF16)],
        compiler_params=_params(("arbitrary",)),
        name="gate_proj",
    )(a, w_t, w_t, b_gate.reshape(DEPTH, 1, GATE_LANES))


def _merge_kernel(ya_ref, yb_ref, yc_ref, w_ref, ga_ref, gb_ref, gc_ref, o_ref, ws):
    @pl.when(pl.program_id(1) == 0)
    def _():
        ws[...] = w_ref[...].astype(BF16)
    acc = _sigmoid(ga_ref[...].astype(F32)) * _dot(ya_ref[...], ws[0])
    acc = acc + _sigmoid(gb_ref[...].astype(F32)) * _dot(yb_ref[...], ws[1])
    acc = acc + _sigmoid(gc_ref[...].astype(F32)) * _dot(yc_ref[...], ws[2])
    o_ref[...] = acc.astype(o_ref.dtype)


def _merge(ya, yb, yc, w_br, layer, p):
    m = ya.shape[0]
    bm, bn = 512, 1024
    g0 = 12 * W_MIX // bn
    gstep = D_MODEL // bn
    yspec = pl.BlockSpec((bm, W_MIX), lambda j, i: (i, 0))
    return pl.pallas_call(
        _merge_kernel,
        grid=(D_MODEL // bn, m // bm),
        in_specs=[yspec, yspec, yspec,
                  pl.BlockSpec((None, 3, W_MIX, bn), lambda j, i: (layer, 0, 0, j)),
                  pl.BlockSpec((bm, bn), lambda j, i: (i, g0 + j)),
                  pl.BlockSpec((bm, bn), lambda j, i: (i, g0 + gstep + j)),
                  pl.BlockSpec((bm, bn), lambda j, i: (i, g0 + 2 * gstep + j))],
        out_specs=pl.BlockSpec((bm, bn), lambda j, i: (i, j)),
        out_shape=jax.ShapeDtypeStruct((m, D_MODEL), BF16),
        scratch_shapes=[pltpu.VMEM((3, W_MIX, bn), BF16)],
        compiler_params=_params(("arbitrary", "arbitrary")),
        name="branch_merge",
    )(ya, yb, yc, w_br, p, p, p)


def _gateprep_kernel(g_ref, pv_ref, gc_ref, grt_ref):
    bm = g_ref.shape[0]
    raw = g_ref[...]
    lane = lax.broadcasted_iota(jnp.int32, (bm, GATE_LANES), 1)
    row = lax.broadcasted_iota(jnp.int32, (bm, GATE_LANES), 0)
    fb = pv_ref[0:1, :]
    dtb = pv_ref[1:2, :]
    alog = pv_ref[2:3, :]
    lf = -_softplus(-(raw + fb))
    beta = _sigmoid(raw)
    gdec = -jnp.exp(alog) * _softplus(raw + dtb)
    act = jnp.where(lane < 16, raw, jnp.where(lane < 32, lf, jnp.where(lane < 48, beta, gdec)))
    seglen = jnp.where(lane < 32, CHUNK_A, CHUNK_B)
    rowmod = row & (seglen - 1)
    pre = act
    suf = act
    s = 1
    while s < max(CHUNK_A, CHUNK_B):
        pre = pre + jnp.where(rowmod >= s, pltpu.roll(pre, s, 0), 0.0)
        suf = suf + jnp.where(rowmod < seglen - s, pltpu.roll(suf, bm - s, 0), 0.0)
        s *= 2
    is_cum = ((lane >= 16) & (lane < 32)) | ((lane >= 48) & (lane < 64))
    backward = ((lane >> 3) & 1) == 1
    out = jnp.where(is_cum, jnp.where(backward, suf, pre), act)
    gc_ref[...] = out
    for t in range(bm // 128):
        grt_ref[:, t * 128:(t + 1) * 128] = out[t * 128:(t + 1) * 128, :].T


def _gateprep(graw, pv):
    m = graw.shape[0]
    bm = 512
    return pl.pallas_call(
        _gateprep_kernel,
        grid=(m // bm,),
        in_specs=[pl.BlockSpec((bm, GATE_LANES), lambda i: (i, 0)),
                  pl.BlockSpec((8, GATE_LANES), lambda i: (0, 0))],
        out_specs=[pl.BlockSpec((bm, GATE_LANES), lambda i: (i, 0)),
                   pl.BlockSpec((GATE_LANES, bm), lambda i: (0, i))],
        out_shape=[jax.ShapeDtypeStruct((m, GATE_LANES), F32),
                   jax.ShapeDtypeStruct((GATE_LANES, m), F32)],
        compiler_params=_params(("arbitrary",)),
        name="gate_prep",
    )(graw, pv)


def _gate_rows(grt, lane0, chunk, chunk_major=True):
    m = grt.shape[1]
    g = grt[lane0:lane0 + 32].reshape(4, N_HEADS, m // chunk, chunk)
    if chunk_major:
        return jnp.pad(jnp.transpose(g, (1, 2, 0, 3)), ((0, 0), (0, 0), (0, 4), (0, 0)))
    return jnp.pad(jnp.transpose(g, (1, 0, 2, 3)), ((0, 0), (0, 4), (0, 0), (0, 0)))


def _select_cols(gc, lane0):
    h = pl.program_id(1)
    src = lax.broadcasted_iota(jnp.int32, (GATE_LANES, GATE_LANES), 0)
    c = lax.broadcasted_iota(jnp.int32, (GATE_LANES, GATE_LANES), 1)
    sel = ((src == lane0 + 8 * c + h) & (c < 4)).astype(BF16)
    return _dot_exact_rhs01(gc, sel)


def _head_out(hsum, g, gate):
    y = hsum * lax.rsqrt(jnp.mean(hsum * hsum, axis=-1, keepdims=True) + EPS) * g
    return y * gate


def _tri_masks(n):
    i = lax.broadcasted_iota(jnp.int32, (n, n), 0)
    j = lax.broadcasted_iota(jnp.int32, (n, n), 1)
    return i, j


def _mlstm_kernel(nseg, has_init, *refs):
    L = CHUNK_A
    refs = list(refs)
    q_ref, k_ref, v_ref, o_ref, gc_ref, rows_ref, hng_ref = refs[:7]
    pos = 7
    if has_init:
        c0_ref, n0_ref, m0_ref = refs[pos:pos + 3]
        pos += 4
    (y_ref, cout_ref, nout_ref, mout_ref,
     hf_ref, hb_ref, gh_ref, sc_ref, kv_ref, ksum_ref, cp_ref, ce_ref) = refs[pos:]
    h_refs = (hf_ref, hb_ref)
    rb = q_ref.shape[0]
    nct = rb // L
    segc = nct // nseg
    head = pl.program_id(1)
    hd = HEAD_DIM
    scale = HEAD_DIM ** -0.5
    ones = jnp.ones((L, hd), BF16)
    gh_ref[...] = _select_cols(gc_ref[...], 0)
    i, j = _tri_masks(L)
    masks = (i >= j, i <= j)
    lanes = (nct, HEAD_DIM)
    for d in range(2):
        ig = rows_ref[d]
        b = rows_ref[2 + d]
        bl = b[:, L - 1:L] if d == 0 else b[:, 0:1]
        sc_ref[d, 0] = jnp.broadcast_to(bl, lanes)
        sc_ref[d, 1] = jnp.broadcast_to(jnp.max(bl - b + ig, axis=1, keepdims=True), lanes)

    def chunk_of(sg, c, d):
        return sg * segc + (c if d == 0 else segc - 1 - c)

    def scan_m(sg, carry):
        if has_init:
            ms = tuple(m0_ref[d, pl.ds(head, 1), :] for d in range(2))
        else:
            ms = (jnp.zeros((1, HEAD_DIM), F32),) * 2

        def step(c, ms):
            out = []
            for d in range(2):
                n = chunk_of(sg, c, d)
                sc_ref[d, 2, pl.ds(n, 1), :] = ms[d]
                m_new = jnp.maximum(sc_ref[d, 0, pl.ds(n, 1), :] + ms[d], sc_ref[d, 1, pl.ds(n, 1), :])
                sc_ref[d, 3, pl.ds(n, 1), :] = m_new
                out.append(m_new)
            return tuple(out)

        ms = lax.fori_loop(0, segc, step, ms)
        for d in range(2):
            mout_ref[sg, d:d + 1, :] = ms[d]
        return carry

    lax.fori_loop(0, nseg, scan_m, 0)

    def increments(g, carry):
        for u in range(MLSTM_GROUP):
            n = g * MLSTM_GROUP + u
            r = pl.multiple_of(n * L, L)
            k = k_ref[pl.ds(r, L), :].astype(F32)
            v1 = jnp.concatenate([v_ref[pl.ds(r, L), :].astype(BF16), ones], axis=1)
            gcol = gh_ref[pl.ds(r, L), :]
            for d in range(2):
                w_t = jnp.exp(sc_ref[d, 0, pl.ds(n, 1), 0:1] - gcol[:, 2 + d:3 + d] + gcol[:, d:d + 1]
                              - sc_ref[d, 3, pl.ds(n, 1), 0:1])
                kw = k * (w_t * scale)
                kv_ref[d, n] = _dot_tn(kw.astype(BF16), v1)
                ksum_ref[d, pl.ds(n, 1), :] = jnp.sum(kw, axis=0, keepdims=True)
        return carry

    lax.fori_loop(0, nct // MLSTM_GROUP, increments, 0)

    def scan_state(sg, carry):
        if has_init:
            nrows = tuple(n0_ref[d, pl.ds(head, 1), :] for d in range(2))
            for d in range(2):
                ce_ref[d, :, 0:hd] = c0_ref[d]
                ce_ref[d, :, hd:2 * hd] = jnp.broadcast_to(nrows[d], (hd, hd)).T
        else:
            nrows = (jnp.zeros((1, hd), F32),) * 2
            ce_ref[...] = jnp.zeros((2, hd, 2 * hd), F32)

        def step(c, nrows):
            out = []
            for d in range(2):
                n = chunk_of(sg, c, d)
                ce = ce_ref[d]
                cp_ref[d, n] = ce.astype(BF16)
                w_s = jnp.exp(sc_ref[d, 0, pl.ds(n, 1), 0:1] + sc_ref[d, 2, pl.ds(n, 1), 0:1]
                              - sc_ref[d, 3, pl.ds(n, 1), 0:1])
                ce_ref[d] = w_s * ce + kv_ref[d, n]
                out.append(w_s * nrows[d] + ksum_ref[d, pl.ds(n, 1), :])
            return tuple(out)

        nrows = lax.fori_loop(0, segc, step, nrows)
        for d in range(2):
            cout_ref[sg, d] = ce_ref[d, :, 0:hd]
            nout_ref[sg, d:d + 1, :] = nrows[d]
        return carry

    lax.fori_loop(0, nseg, scan_state, 0)

    def outputs(g, carry):
        for u in range(MLSTM_GROUP):
            n = g * MLSTM_GROUP + u
            r = pl.multiple_of(n * L, L)
            qb = q_ref[pl.ds(r, L), :].astype(BF16)
            v1 = jnp.concatenate([v_ref[pl.ds(r, L), :].astype(BF16), ones], axis=1)
            qk = _dot_nt(qb, (k_ref[pl.ds(r, L), :].astype(F32) * scale).astype(BF16))
            gcol = gh_ref[pl.ds(r, L), :]
            for d in range(2):
                bcol = jnp.broadcast_to(gcol[:, 2 + d:3 + d], (L, L))
                m_prev = sc_ref[d, 2, pl.ds(n, 1), 0:1]
                logd = jnp.where(masks[d], bcol - rows_ref[2 + d, pl.ds(n, 1), :] + rows_ref[d, pl.ds(n, 1), :],
                                 NEG_BIG)
                m_i = jnp.maximum(bcol + m_prev, jnp.broadcast_to(jnp.max(logd, axis=1, keepdims=True), (L, L)))
                s = qk * jnp.exp(logd - m_i)
                w_prev = jnp.exp(bcol + m_prev - m_i)
                qc = _dot(qb, cp_ref[d, n])
                sv = _dot(s.astype(BF16), v1)
                num = w_prev * qc[:, 0:hd] + sv[:, 0:hd]
                den = w_prev * qc[:, hd:2 * hd] + sv[:, hd:2 * hd]
                h_refs[d][pl.ds(r, L), :] = num / jnp.maximum(jnp.abs(den), jnp.exp(-m_i))
        return carry

    lax.fori_loop(0, nct // MLSTM_GROUP, outputs, 0)
    y = _head_out(hf_ref[...] + hb_ref[...], hng_ref[...], _sigmoid(o_ref[...].astype(F32)))
    y_ref[...] = y.astype(y_ref.dtype)


def _ret_kernel(nseg, has_init, use_rope, *refs):
    L = CHUNK_C
    refs = list(refs)
    q_ref, k_ref, v_ref, g_ref, lg_ref, hng_ref = refs[:6]
    pos = 6
    if use_rope:
        cos_ref, sin_ref = refs[pos:pos + 2]
        pos += 2
    if has_init:
        s0_ref = refs[pos]
        pos += 2
    y_ref, sout_ref, o_ref, qs_ref, ks_ref, kv_ref, sp_ref = refs[pos:]
    rb = q_ref.shape[0]
    nct = rb // L
    segc = nct // nseg
    hd = HEAD_DIM

    q = q_ref[...].astype(F32) * HEAD_DIM ** -0.5
    k = k_ref[...].astype(F32)
    if use_rope:
        lane = lax.broadcasted_iota(jnp.int32, (rb, HEAD_DIM), 1)
        cos = cos_ref[...]
        sin = sin_ref[...]
        half = HEAD_DIM // 2

        def rope(x):
            rot = pltpu.roll(x, half, 1)
            return x * cos + jnp.where(lane < half, -rot, rot) * sin
        q = rope(q)
        k = rope(k)
    qs_ref[...] = q
    ks_ref[...] = k

    lg_f = lg_ref[0:1, 0:1]
    lg_b = lg_ref[1:2, 0:1]
    i, j = _tri_masks(L)
    dij = (i - j).astype(F32)
    decay_f = jnp.where(i >= j, jnp.exp(lg_f * jnp.maximum(dij, 0.0)), 0.0)
    decay_b = jnp.where(i <= j, jnp.exp(lg_b * jnp.maximum(-dij, 0.0)), 0.0)
    pos_c = lax.broadcasted_iota(jnp.int32, (L, 1), 0).astype(F32)
    qdec_f = jnp.exp(lg_f * (pos_c + 1.0))
    kend_f = jnp.exp(lg_f * (L - 1.0 - pos_c))
    qdec_b = jnp.exp(lg_b * (L - pos_c))
    kend_b = jnp.exp(lg_b * pos_c)
    cdec_f = jnp.exp(lg_f * L)
    cdec_b = jnp.exp(lg_b * L)

    decay = decay_f + decay_b

    def increments(g, carry):
        for u in range(RET_GROUP):
            n = g * RET_GROUP + u
            r = pl.multiple_of(n * L, L)
            kc = ks_ref[pl.ds(r, L), :]
            ke = jnp.concatenate([kc * kend_f, kc * kend_b], axis=1).astype(BF16)
            kv_ref[n] = _dot_tn(ke, v_ref[pl.ds(r, L), :].astype(BF16))
        return carry

    lax.fori_loop(0, nct // RET_GROUP, increments, 0)

    def scan_state(sg, carry):
        if has_init:
            st = (s0_ref[0], s0_ref[1])
        else:
            st = (jnp.zeros((hd, hd), F32),) * 2

        def step(c, st):
            nf = sg * segc + c
            nb = sg * segc + (segc - 1 - c)
            sp_ref[nf, 0:hd, :] = st[0].astype(BF16)
            sp_ref[nb, hd:2 * hd, :] = st[1].astype(BF16)
            return (cdec_f * st[0] + kv_ref[nf, 0:hd, :], cdec_b * st[1] + kv_ref[nb, hd:2 * hd, :])

        st = lax.fori_loop(0, segc, step, st)
        sout_ref[sg, 0] = st[0]
        sout_ref[sg, 1] = st[1]
        return carry

    lax.fori_loop(0, nseg, scan_state, 0)

    def outputs(g, carry):
        for u in range(RET_GROUP):
            n = g * RET_GROUP + u
            r = pl.multiple_of(n * L, L)
            qc = qs_ref[pl.ds(r, L), :]
            qk = _dot_nt(qc.astype(BF16), ks_ref[pl.ds(r, L), :].astype(BF16))
            inner = _dot((qk * decay).astype(BF16), v_ref[pl.ds(r, L), :].astype(BF16))
            qd = jnp.concatenate([qc * qdec_f, qc * qdec_b], axis=1).astype(BF16)
            o_ref[pl.ds(r, L), :] = inner + _dot(qd, sp_ref[n])
        return carry

    lax.fori_loop(0, nct // RET_GROUP, outputs, 0)
    gate = g_ref[...].astype(F32)
    y = _head_out(o_ref[...], hng_ref[...], gate * _sigmoid(gate))
    y_ref[...] = y.astype(y_ref.dtype)


def _retention(p, lgrows, hng, rope, y_prev, s0, *, blk0, nblk, nseg, layer, col0, rb=ROW_GROUP):
    m = p.shape[0]
    assert (rb // CHUNK_C) % RET_GROUP == 0
    colspec = lambda c0: pl.BlockSpec((rb, HEAD_DIM), lambda b, h: (blk0 + b, c0 + h))
    in_specs = [colspec(col0), colspec(col0 + 8), colspec(col0 + 16), colspec(col0 + 24),
                pl.BlockSpec((None, 8, HEAD_DIM), lambda b, h: (h, 0, 0)),
                pl.BlockSpec((1, HEAD_DIM), lambda b, h: (0, 0))]
    args = [p, p, p, p, lgrows, hng]
    if rope is not None:
        in_specs += [pl.BlockSpec((rb, HEAD_DIM), lambda b, h: (0, 0))] * 2
        args += list(rope)
    aliases = {}
    if s0 is not None:
        in_specs += [pl.BlockSpec((None, None, 2, None, HEAD_DIM, HEAD_DIM), lambda b, h: (b, layer, 0, h, 0, 0)),
                     pl.BlockSpec(memory_space=pl.ANY)]
        args += [s0, y_prev]
        aliases = {len(args) - 1: 0}
    nsq = nblk * nseg
    return pl.pallas_call(
        functools.partial(_ret_kernel, nseg, s0 is not None, rope is not None),
        grid=(nblk, N_HEADS),
        in_specs=in_specs,
        out_specs=[pl.BlockSpec((rb, HEAD_DIM), lambda b, h: (blk0 + b, h)),
                   pl.BlockSpec((nseg, 2, None, HEAD_DIM, HEAD_DIM), lambda b, h: (b, 0, h, 0, 0))],
        out_shape=[jax.ShapeDtypeStruct((m, W_MIX), BF16),
                   jax.ShapeDtypeStruct((nsq, 2, N_HEADS, HEAD_DIM, HEAD_DIM), F32)],
        scratch_shapes=[pltpu.VMEM((rb, HEAD_DIM), F32), pltpu.VMEM((rb, HEAD_DIM), F32),
                        pltpu.VMEM((rb, HEAD_DIM), F32),
                        pltpu.VMEM((rb // CHUNK_C, 2 * HEAD_DIM, HEAD_DIM), F32),
                        pltpu.VMEM((rb // CHUNK_C, 2 * HEAD_DIM, HEAD_DIM), BF16)],
        input_output_aliases=aliases,
        compiler_params=_params(("arbitrary", "arbitrary")),
        name="retention",
    )(*args)


def _inv_unit_triangular(mats, eye, i, j):
    n = eye.shape[0]
    xs = [eye - jnp.where((i >> 1) == (j >> 1), a, 0.0) for a in mats]
    s, sh = 2, 1
    while s < n:
        join = ((i >> (sh + 1)) == (j >> (sh + 1))) & ((i >> sh) != (j >> sh))
        xb = [x.astype(BF16) for x in xs]
        mt = [_dot(jnp.where(join, a, 0.0).astype(BF16), b).astype(BF16) for a, b in zip(mats, xb)]
        xs = [x - _dot(b, t) for x, b, t in zip(xs, xb, mt)]
        s, sh = 2 * s, sh + 1
    rs = [(eye - x - _dot_x3(a, x)).astype(BF16) for a, x in zip(mats, xs)]
    return [x + _dot(x.astype(BF16), r) for x, r in zip(xs, rs)]


def _delta_kernel(nseg, has_init, *refs):
    L = CHUNK_B
    refs = list(refs)
    q_ref, k_ref, v_ref, z_ref, cwq_ref, cwk_ref, cwv_ref, gc_ref, rows_ref, hng_ref = refs[:10]
    pos = 10
    if has_init:
        s0_ref = refs[pos]
        pos += 2
    (y_ref, sout_ref, qs_ref, ks_ref, vs_ref, gh_ref, u0f_ref, u0b_ref, wkf_ref, wkb_ref,
     qkf_ref, qkb_ref, qdf_ref, qdb_ref, ketf_ref, ketb_ref, sf_ref, sb_ref) = refs[pos:]
    of_ref, ob_ref = qs_ref, vs_ref
    rb = q_ref.shape[0]
    tseg = rb // nseg
    segc = tseg // L
    nchunks = rb // L

    row = lax.broadcasted_iota(jnp.int32, (rb, HEAD_DIM), 0)
    rowmod = row % tseg
    first = rowmod == 0
    last = rowmod == tseg - 1

    def conv_silu(x_ref, w_ref):
        x = x_ref[...].astype(F32)
        prev = jnp.where(first, 0.0, pltpu.roll(x, 1, 0))
        nxt = jnp.where(last, 0.0, pltpu.roll(x, rb - 1, 0))
        y = w_ref[0:1, :] * prev + w_ref[1:2, :] * x + w_ref[2:3, :] * nxt
        return y * _sigmoid(y)

    def l2n(x):
        return x * lax.rsqrt(jnp.sum(x * x, axis=-1, keepdims=True) + EPS)

    qs_ref[...] = l2n(conv_silu(q_ref, cwq_ref)) * HEAD_DIM ** -0.5
    ks_ref[...] = l2n(conv_silu(k_ref, cwk_ref))
    vs_ref[...] = conv_silu(v_ref, cwv_ref)
    gh_ref[...] = _select_cols(gc_ref[...], 32)

    i, j = _tri_masks(L)
    eye = (i == j).astype(F32)
    masks = ((i >= j, i > j), (i <= j, i < j))

    dir_refs = ((u0f_ref, wkf_ref, qkf_ref, qdf_ref, ketf_ref), (u0b_ref, wkb_ref, qkb_ref, qdb_ref, ketb_ref))

    def g_last(rw, d):
        return rw[2:3, L - 1:L] if d == 0 else rw[3:4, 0:1]

    def prep(gi, carry):
        chains = []
        for u in range(PREP_GROUP):
            c = gi * PREP_GROUP + u
            r = pl.multiple_of(c * L, L)
            qc = qs_ref[pl.ds(r, L), :]
            kc = ks_ref[pl.ds(r, L), :]
            kb = kc.astype(BF16)
            kk = _dot_nt(kb, kb)
            qk = _dot_nt(qc.astype(BF16), kb)
            gcol = gh_ref[pl.ds(r, L), :]
            rw = rows_ref[c]
            for d in range(2):
                incl, strict = masks[d]
                g_c = gcol[:, 2 + d:3 + d]
                decay = jnp.where(incl, jnp.exp(jnp.where(incl, g_c - rw[2 + d:3 + d, :], 0.0)), 0.0)
                dir_refs[d][2][pl.ds(r, L), :] = (qk * decay).astype(BF16)
                dir_refs[d][3][pl.ds(r, L), :] = (qc * jnp.exp(g_c)).astype(BF16)
                dir_refs[d][4][c] = (kc * jnp.exp(g_last(rw, d) - g_c)).T.astype(BF16)
                chains.append((r, d, jnp.where(strict, kk * decay, 0.0) * gcol[:, d:d + 1]))
        invs = _inv_unit_triangular([a for _, _, a in chains], eye, i, j)
        for (r, d, _), t in zip(chains, invs):
            gcol = gh_ref[pl.ds(r, L), :]
            beta_c = gcol[:, d:d + 1]
            bg = beta_c * jnp.exp(gcol[:, 2 + d:3 + d])
            tb = t.astype(BF16)
            dir_refs[d][0][pl.ds(r, L), :] = _dot(tb, (beta_c * vs_ref[pl.ds(r, L), :]).astype(BF16))
            dir_refs[d][1][pl.ds(r, L), :] = _dot(tb, (bg * ks_ref[pl.ds(r, L), :]).astype(BF16)).astype(BF16)
        return carry

    lax.fori_loop(0, nchunks // PREP_GROUP, prep, 0)

    def one(c, d, s_ref, o_ref):
        u0_ref, wk_ref, qkd_ref, qd_ref, ket_ref = dir_refs[d]
        r = pl.multiple_of(c * L, L)
        s_prev = s_ref[...]
        sb = s_prev.astype(BF16)
        ub = (u0_ref[pl.ds(r, L), :] - _dot(wk_ref[pl.ds(r, L), :], sb)).astype(BF16)
        o_ref[pl.ds(r, L), :] = _dot(qd_ref[pl.ds(r, L), :], sb) + _dot(qkd_ref[pl.ds(r, L), :], ub)
        s_ref[...] = jnp.exp(g_last(rows_ref[c], d)) * s_prev + _dot(ket_ref[c], ub)

    def segment(sg, carry):
        if has_init:
            sf_ref[...] = s0_ref[0]
            sb_ref[...] = s0_ref[1]
        else:
            sf_ref[...] = jnp.zeros((HEAD_DIM, HEAD_DIM), F32)
            sb_ref[...] = jnp.zeros((HEAD_DIM, HEAD_DIM), F32)

        def step(c, carry2):
            one(sg * segc + c, 0, sf_ref, of_ref)
            one(sg * segc + (segc - 1 - c), 1, sb_ref, ob_ref)
            return carry2

        lax.fori_loop(0, segc, step, 0)
        sout_ref[sg, 0] = sf_ref[...]
        sout_ref[sg, 1] = sb_ref[...]
        return carry

    lax.fori_loop(0, nseg, segment, 0)
    gate = z_ref[...].astype(F32)
    y = _head_out(of_ref[...] + ob_ref[...], hng_ref[...], gate * _sigmoid(gate))
    y_ref[...] = y.astype(y_ref.dtype)


def _delta(p, conv_w, gc, rows, hng, y_prev, s0, *, blk0, nblk, nseg, layer, col0, rb=ROW_GROUP):
    m = p.shape[0]
    cb = rb // CHUNK_B
    assert cb % PREP_GROUP == 0
    colspec = lambda c0: pl.BlockSpec((rb, HEAD_DIM), lambda b, h: (blk0 + b, c0 + h))
    cwspec = lambda c0: pl.BlockSpec((None, CONV_K, HEAD_DIM), lambda b, h: (layer, 0, c0 + h))
    in_specs = [colspec(col0), colspec(col0 + 8), colspec(col0 + 16), colspec(col0 + 24),
                cwspec(0), cwspec(8), cwspec(16),
                pl.BlockSpec((rb, GATE_LANES), lambda b, h: (blk0 + b, 0)),
                pl.BlockSpec((None, cb, 8, CHUNK_B), lambda b, h: (h, blk0 + b, 0, 0)),
                pl.BlockSpec((1, HEAD_DIM), lambda b, h: (0, 0))]
    args = [p, p, p, p, conv_w, conv_w, conv_w, gc, rows, hng]
    aliases = {}
    if s0 is not None:
        in_specs += [pl.BlockSpec((None, None, 2, None, HEAD_DIM, HEAD_DIM), lambda b, h: (b, layer, 0, h, 0, 0)),
                     pl.BlockSpec(memory_space=pl.ANY)]
        args += [s0, y_prev]
        aliases = {len(args) - 1: 0}
    nsq = nblk * nseg
    big = lambda dt: pltpu.VMEM((rb, HEAD_DIM), dt)
    return pl.pallas_call(
        functools.partial(_delta_kernel, nseg, s0 is not None),
        grid=(nblk, N_HEADS),
        in_specs=in_specs,
        out_specs=[pl.BlockSpec((rb, HEAD_DIM), lambda b, h: (blk0 + b, h)),
                   pl.BlockSpec((nseg, 2, None, HEAD_DIM, HEAD_DIM), lambda b, h: (b, 0, h, 0, 0))],
        out_shape=[jax.ShapeDtypeStruct((m, W_MIX), BF16),
                   jax.ShapeDtypeStruct((nsq, 2, N_HEADS, HEAD_DIM, HEAD_DIM), F32)],
        scratch_shapes=[big(F32), big(F32), big(F32), pltpu.VMEM((rb, GATE_LANES), F32),
                        big(F32), big(F32), big(BF16), big(BF16),
                        pltpu.VMEM((rb, CHUNK_B), BF16), pltpu.VMEM((rb, CHUNK_B), BF16),
                        big(BF16), big(BF16),
                        pltpu.VMEM((cb, HEAD_DIM, CHUNK_B), BF16), pltpu.VMEM((cb, HEAD_DIM, CHUNK_B), BF16),
                        pltpu.VMEM((HEAD_DIM, HEAD_DIM), F32), pltpu.VMEM((HEAD_DIM, HEAD_DIM), F32)],
        input_output_aliases=aliases,
        compiler_params=_params(("arbitrary", "arbitrary")),
        name="deltanet",
    )(*args)


def _state_specs(nseg):
    specs = [pl.BlockSpec((nseg, 2, None, HEAD_DIM, HEAD_DIM), lambda b, h: (b, 0, h, 0, 0)),
             pl.BlockSpec((None, nseg, 2, HEAD_DIM), lambda b, h: (h, b, 0, 0)),
             pl.BlockSpec((None, nseg, 2, HEAD_DIM), lambda b, h: (h, b, 0, 0))]
    return specs


def _mlstm(p, gc, rows, hng, y_prev, init, *, blk0, nblk, nseg, layer, rb=ROW_GROUP):
    m = p.shape[0]
    cb = rb // CHUNK_A
    assert cb % MLSTM_GROUP == 0
    colspec = lambda c0: pl.BlockSpec((rb, HEAD_DIM), lambda b, h: (blk0 + b, c0 + h))
    in_specs = [colspec(0), colspec(8), colspec(16), colspec(24),
                pl.BlockSpec((rb, GATE_LANES), lambda b, h: (blk0 + b, 0)),
                pl.BlockSpec((None, 8, cb, CHUNK_A), lambda b, h: (h, 0, blk0 + b, 0)),
                pl.BlockSpec((1, HEAD_DIM), lambda b, h: (0, 0))]
    args = [p, p, p, p, gc, rows, hng]
    if init is not None:
        c0, n0, m0 = init
        in_specs += [pl.BlockSpec((None, None, 2, None, HEAD_DIM, HEAD_DIM), lambda b, h: (b, layer, 0, h, 0, 0)),
                     pl.BlockSpec((None, None, 2, N_HEADS, HEAD_DIM), lambda b, h: (b, layer, 0, 0, 0)),
                     pl.BlockSpec((None, None, 2, N_HEADS, HEAD_DIM), lambda b, h: (b, layer, 0, 0, 0)),
                     pl.BlockSpec(memory_space=pl.ANY)]
        args += [c0, n0, m0, y_prev]
        aliases = {len(args) - 1: 0}
    else:
        aliases = {}
    nsq = nblk * nseg
    out_shape = [jax.ShapeDtypeStruct((m, W_MIX), BF16),
                 jax.ShapeDtypeStruct((nsq, 2, N_HEADS, HEAD_DIM, HEAD_DIM), F32),
                 jax.ShapeDtypeStruct((N_HEADS, nsq, 2, HEAD_DIM), F32),
                 jax.ShapeDtypeStruct((N_HEADS, nsq, 2, HEAD_DIM), F32)]
    out_specs = [pl.BlockSpec((rb, HEAD_DIM), lambda b, h: (blk0 + b, h))] + _state_specs(nseg)
    return pl.pallas_call(
        functools.partial(_mlstm_kernel, nseg, init is not None),
        grid=(nblk, N_HEADS),
        in_specs=in_specs, out_specs=out_specs, out_shape=out_shape,
        scratch_shapes=[pltpu.VMEM((rb, HEAD_DIM), F32), pltpu.VMEM((rb, HEAD_DIM), F32),
                        pltpu.VMEM((rb, GATE_LANES), F32),
                        pltpu.VMEM((2, 4, cb, HEAD_DIM), F32),
                        pltpu.VMEM((2, cb, HEAD_DIM, 2 * HEAD_DIM), F32), pltpu.VMEM((2, cb, HEAD_DIM), F32),
                        pltpu.VMEM((2, cb, HEAD_DIM, 2 * HEAD_DIM), BF16),
                        pltpu.VMEM((2, HEAD_DIM, 2 * HEAD_DIM), F32)],
        input_output_aliases=aliases,
        compiler_params=_params(("arbitrary", "arbitrary")),
        name="mlstm",
    )(*args)


_MAIN_COLS = ((0, 4 * W_MIX), (4 * W_MIX + 32, 8 * W_MIX + 32), (8 * W_MIX + 64, 12 * W_MIX + 64 + 3 * D_MODEL))
_GATE_COLS = ((4 * W_MIX, 4 * W_MIX + 32), (8 * W_MIX + 32, 8 * W_MIX + 64))
COL_A, COL_B, COL_C = 0, 32, 64


def _take_cols(w, ranges):
    return jnp.concatenate([w[..., a:b] for a, b in ranges], axis=-1)


def _rope_tables(t):
    pos = jnp.arange(t)
    pos_r = (pos // GRID_W).astype(F32)
    pos_c = (pos % GRID_W).astype(F32)
    nf = HEAD_DIM // 4
    freqs = ROPE_BASE ** (-jnp.arange(nf, dtype=F32) / nf)
    ang = jnp.concatenate([pos_r[:, None] * freqs, pos_c[:, None] * freqs], axis=-1)
    ang = jnp.concatenate([ang, ang], axis=-1)
    return jnp.cos(ang), jnp.sin(ang)


def kernel(x_prompt, x_sample, state_mlstm_C, state_mlstm_n, state_mlstm_m, state_delta_S, state_ret_S,
           c, c_ctx, norm_g, final_norm_g, w_ada, b_ada, w_in, b_in, mlstm_f_bias, conv_w, delta_A_log,
           delta_dt_bias, ret_log_gamma, head_norm_g, w_br, w_out, ffn_w13, ffn_w2):
    bp, sp, d = x_prompt.shape
    bs, ss, _ = x_sample.shape
    assert d == D_MODEL and bp * sp == ROW_GROUP and ss == ROW_GROUP
    n_prompt_blk, n_sample_blk = 1, bs
    x = jnp.concatenate([x_prompt.reshape(-1, d), x_sample.reshape(-1, d)], axis=0)

    cv8 = jnp.zeros((8, d), F32).at[0].set(c_ctx).at[1:1 + bs].set(c)
    mod = _ada(cv8, w_ada, b_ada)[:, :1 + bs].reshape(DEPTH, 1 + bs, N_MOD, 1, d)
    rope = _rope_tables(ss)
    m0_pad = jnp.broadcast_to(state_mlstm_m[..., None], state_mlstm_m.shape + (HEAD_DIM,))

    w_t = jnp.swapaxes(w_in, 1, 2)
    b_main = _take_cols(b_in, _MAIN_COLS)
    pad = GATE_LANES - 64
    b_gate = jnp.pad(_take_cols(b_in, _GATE_COLS), ((0, 0), (0, pad)))

    new_c, new_n, new_m, new_sd, new_sr = [], [], [], [], []
    for l in range(DEPTH):
        md = lambda i: mod[l, :, i]
        h = _normmod(x, norm_g[l, 0], md(0), md(1))
        x = _mm_res(_mm_swiglu(h, ffn_w13, (l, 0)), ffn_w2, (l, 0), x, md(2), 0.5, 1024, 512)

        h = _normmod(x, norm_g[l, 1], md(3), md(4))
        p = _in_proj(h, w_t, b_main, l)
        graw = _gate_proj(h, w_t, b_gate, l)
        pv = jnp.zeros((8, GATE_LANES), F32)
        pv = pv.at[0, 16:32].set(mlstm_f_bias[l].reshape(-1))
        pv = pv.at[1, 48:64].set(delta_dt_bias[l].reshape(-1))
        pv = pv.at[2, 48:64].set(delta_A_log[l].reshape(-1))
        gc, grt = _gateprep(graw, pv)
        rows_a = _gate_rows(grt, 0, CHUNK_A, chunk_major=False)
        rows_b = _gate_rows(grt, 32, CHUNK_B)
        hng = head_norm_g[l].reshape(3, 1, HEAD_DIM)
        lgrows = jnp.zeros((N_HEADS, 8, HEAD_DIM), F32).at[:, 0:2, :].set(
            jnp.broadcast_to(ret_log_gamma[l].T[:, :, None], (N_HEADS, 2, HEAD_DIM)))
        prompt = dict(blk0=0, nblk=n_prompt_blk, nseg=bp, layer=l)
        sample = dict(blk0=n_prompt_blk, nblk=n_sample_blk, nseg=1, layer=l)

        ya, c_new, n_new, m_new = _mlstm(p, gc, rows_a, hng[0], None, None, **prompt)
        ya = _mlstm(p, gc, rows_a, hng[0], ya, (state_mlstm_C, state_mlstm_n, m0_pad), **sample)[0]
        yb, sd_new = _delta(p, conv_w, gc, rows_b, hng[1], None, None, col0=COL_B, **prompt)
        yb = _delta(p, conv_w, gc, rows_b, hng[1], yb, state_delta_S, col0=COL_B, **sample)[0]
        yc, sr_new = _retention(p, lgrows, hng[2], None, None, None, col0=COL_C, **prompt)
        yc = _retention(p, lgrows, hng[2], rope, yc, state_ret_S, col0=COL_C, **sample)[0]
        new_c.append(c_new)
        new_n.append(jnp.transpose(n_new, (1, 2, 0, 3)))
        new_m.append(jnp.transpose(m_new[..., 0], (1, 2, 0)))
        new_sd.append(sd_new)
        new_sr.append(sr_new)

        x = _mm_res(_merge(ya, yb, yc, w_br, l, p), w_out, (l,), x, md(5), 1.0, 1024, 1024)
        h = _normmod(x, norm_g[l, 2], md(6), md(7))
        x = _mm_res(_mm_swiglu(h, ffn_w13, (l, 1)), ffn_w2, (l, 1), x, md(8), 0.5, 1024, 512)

    y_prompt = _final_norm(x, final_norm_g, 0, bp * sp).reshape(bp, sp, d)
    y_sample = _final_norm(x, final_norm_g, bp * sp, bs * ss).reshape(bs, ss, d)
    stack = lambda xs: jnp.stack(xs, axis=1)
    return (y_prompt, y_sample, stack(new_c), stack(new_n), stack(new_m), stack(new_sd), stack(new_sr))
```

```python
import functools

import jax
import jax.numpy as jnp
import numpy as np
from jax import lax
from jax.experimental import pallas as pl
from jax.experimental.pallas import tpu as pltpu

F32 = jnp.float32
BF16 = jnp.bfloat16

D_MODEL = 2048
DEPTH = 2
N_HEADS = 8
HEAD_DIM = 128
W_MIX = N_HEADS * HEAD_DIM
D_FF = 4096
N_MOD = 9
CONV_K = 3
GRID_W = 64
ROPE_BASE = 10000.0
EPS = 1e-6
ROW_GROUP = 4096
N_MAIN = 12 * W_MIX + 3 * D_MODEL
GATE_LANES = 128
CHUNK_A = 128
CHUNK_B = 128
RET_GROUP = 4
MLSTM_GROUP = 2
PREP_GROUP = 8
CHUNK_C = 128
NEG_BIG = -1e30
VMEM_LIMIT = 56 * 1024 * 1024


def _params(sem):
    return pltpu.CompilerParams(dimension_semantics=sem, vmem_limit_bytes=VMEM_LIMIT)


def _dot(a, b):
    return jnp.dot(a, b, preferred_element_type=F32)


def _dot_nt(a, b):
    return lax.dot_general(a, b, (((1,), (1,)), ((), ())), preferred_element_type=F32)


def _dot_tn(a, b):
    return lax.dot_general(a, b, (((0,), (0,)), ((), ())), preferred_element_type=F32)


def _split3(x):
    hi = x.astype(BF16)
    r1 = x - hi.astype(F32)
    mid = r1.astype(BF16)
    lo = (r1 - mid.astype(F32)).astype(BF16)
    return hi, mid, lo


def _dot_exact_rhs01(x, sel):
    hi, mid, lo = _split3(x)
    return _dot(hi, sel) + _dot(mid, sel) + _dot(lo, sel)


def _dot_x3(a, b):
    ah = a.astype(BF16)
    al = (a - ah.astype(F32)).astype(BF16)
    bh = b.astype(BF16)
    bl = (b - bh.astype(F32)).astype(BF16)
    return _dot(ah, bh) + _dot(ah, bl) + _dot(al, bh)


def _sigmoid(x):
    return 0.5 * jnp.tanh(0.5 * x) + 0.5


def _softplus(x):
    return jnp.maximum(x, 0.0) + jnp.log(1.0 + jnp.exp(-jnp.abs(x)))


def _ada_kernel(cv_ref, w_ref, b_ref, o_ref):
    cv = cv_ref[...]
    s = (cv * _sigmoid(cv)).astype(BF16)
    o_ref[...] = _dot(s, w_ref[...].astype(BF16)) + b_ref[...]


def _ada(cv8, w_ada, b_ada):
    n = w_ada.shape[-1]
    bn = 1024
    return pl.pallas_call(
        _ada_kernel,
        grid=(DEPTH, n // bn),
        in_specs=[pl.BlockSpec((8, D_MODEL), lambda l, j: (0, 0)),
                  pl.BlockSpec((None, D_MODEL, bn), lambda l, j: (l, 0, j)),
                  pl.BlockSpec((None, 1, bn), lambda l, j: (l, 0, j))],
        out_specs=pl.BlockSpec((None, 8, bn), lambda l, j: (l, 0, j)),
        out_shape=jax.ShapeDtypeStruct((DEPTH, 8, n), F32),
        compiler_params=_params(("arbitrary", "arbitrary")),
        name="adaln",
    )(cv8, w_ada, b_ada.reshape(DEPTH, 1, n))


def _normmod_kernel(x_ref, g_ref, sh_ref, sc_ref, o_ref):
    x = x_ref[...]
    y = x * lax.rsqrt(jnp.mean(x * x, axis=-1, keepdims=True) + EPS) * g_ref[...]
    o_ref[...] = (y * (1.0 + sc_ref[...]) + sh_ref[...]).astype(o_ref.dtype)


def _normmod(x, g, shift, scale):
    m = x.shape[0]
    bm = 512
    grp = ROW_GROUP // bm
    return pl.pallas_call(
        _normmod_kernel,
        grid=(m // bm,),
        in_specs=[pl.BlockSpec((bm, D_MODEL), lambda i: (i, 0)),
                  pl.BlockSpec((1, D_MODEL), lambda i: (0, 0)),
                  pl.BlockSpec((None, 1, D_MODEL), lambda i: (i // grp, 0, 0)),
                  pl.BlockSpec((None, 1, D_MODEL), lambda i: (i // grp, 0, 0))],
        out_specs=pl.BlockSpec((bm, D_MODEL), lambda i: (i, 0)),
        out_shape=jax.ShapeDtypeStruct((m, D_MODEL), BF16),
        compiler_params=_params(("arbitrary",)),
        name="normmod",
    )(x, g.reshape(1, D_MODEL), shift, scale)


def _rmsnorm_kernel(x_ref, g_ref, o_ref):
    x = x_ref[...]
    o_ref[...] = x * lax.rsqrt(jnp.mean(x * x, axis=-1, keepdims=True) + EPS) * g_ref[...]


def _final_norm(x, g, row0, rows):
    bm = 512
    off = row0 // bm
    return pl.pallas_call(
        _rmsnorm_kernel,
        grid=(rows // bm,),
        in_specs=[pl.BlockSpec((bm, D_MODEL), lambda i: (i + off, 0)),
                  pl.BlockSpec((1, D_MODEL), lambda i: (0, 0))],
        out_specs=pl.BlockSpec((bm, D_MODEL), lambda i: (i, 0)),
        out_shape=jax.ShapeDtypeStruct((rows, D_MODEL), F32),
        compiler_params=_params(("arbitrary",)),
        name="final_norm",
    )(x, g.reshape(1, D_MODEL))


def _swiglu_kernel(h_ref, wa_ref, wb_ref, o_ref, was, wbs):
    @pl.when(pl.program_id(1) == 0)
    def _():
        was[...] = wa_ref[...].astype(BF16)
        wbs[...] = wb_ref[...].astype(BF16)
    h = h_ref[...]
    a = _dot(h, was[...])
    b = _dot(h, wbs[...])
    o_ref[...] = (a * _sigmoid(a) * b).astype(o_ref.dtype)


def _wspec(widx, k, bn, col=lambda j: j):
    lead = tuple(widx)
    return pl.BlockSpec((None,) * len(lead) + (k, bn), lambda j, i: lead + (0, col(j)))


def _mm_swiglu(h, w13, widx):
    m, k = h.shape
    f = w13.shape[-1] // 2
    bm, bn = 1024, 512
    nb = f // bn
    return pl.pallas_call(
        _swiglu_kernel,
        grid=(nb, m // bm),
        in_specs=[pl.BlockSpec((bm, k), lambda j, i: (i, 0)),
                  _wspec(widx, k, bn),
                  _wspec(widx, k, bn, lambda j: j + nb)],
        out_specs=pl.BlockSpec((bm, bn), lambda j, i: (i, j)),
        out_shape=jax.ShapeDtypeStruct((m, f), BF16),
        scratch_shapes=[pltpu.VMEM((k, bn), BF16), pltpu.VMEM((k, bn), BF16)],
        compiler_params=_params(("arbitrary", "arbitrary")),
        name="ffn_up",
    )(h, w13, w13)


def _mm_res_kernel(coef, a_ref, w_ref, x_ref, s_ref, o_ref, ws):
    @pl.when(pl.program_id(1) == 0)
    def _():
        ws[...] = w_ref[...].astype(BF16)
    acc = _dot(a_ref[...], ws[...])
    o_ref[...] = x_ref[...] + (coef * s_ref[...]) * acc


def _mm_res(a, w, widx, x, s, coef, bm, bn):
    m, k = a.shape
    n = w.shape[-1]
    grp = ROW_GROUP // bm
    return pl.pallas_call(
        functools.partial(_mm_res_kernel, coef),
        grid=(n // bn, m // bm),
        in_specs=[pl.BlockSpec((bm, k), lambda j, i: (i, 0)),
                  _wspec(widx, k, bn),
                  pl.BlockSpec((bm, bn), lambda j, i: (i, j)),
                  pl.BlockSpec((None, 1, bn), lambda j, i: (i // grp, 0, j))],
        out_specs=pl.BlockSpec((bm, bn), lambda j, i: (i, j)),
        out_shape=jax.ShapeDtypeStruct((m, n), F32),
        scratch_shapes=[pltpu.VMEM((k, bn), BF16)],
        compiler_params=_params(("arbitrary", "arbitrary")),
        name="mm_residual",
    )(a, w, x, s)


IN_BN = 1024
_IN_SEGMENTS = ((0, 4, 0), (4, 8, 32), (8, 18, 64))


def _in_proj_kernel(a_ref, wa_ref, wb_ref, b_ref, o_ref, ws):
    j = pl.program_id(0)
    first_row_tile = pl.program_id(1) == 0
    rows = 256
    for lo, hi, shift in _IN_SEGMENTS:
        @pl.when(first_row_tile & (j >= lo) & (j < hi))
        def _(shift=shift):
            for r in range(0, IN_BN - shift, rows):
                n = min(rows, IN_BN - shift - r)
                ws[r:r + n, :] = wa_ref[shift + r:shift + r + n, :].astype(BF16)
            if shift:
                ws[IN_BN - shift:IN_BN, :] = wb_ref[0:shift, :].astype(BF16)
    o_ref[...] = (_dot_nt(a_ref[...], ws[...]) + b_ref[...]).astype(o_ref.dtype)


def _in_proj(a, w_t, b_main, layer):
    m, k = a.shape
    bm, bn = 1024, IN_BN
    assert N_MAIN // bn == _IN_SEGMENTS[-1][1]
    return pl.pallas_call(
        _in_proj_kernel,
        grid=(N_MAIN // bn, m // bm),
        in_specs=[pl.BlockSpec((bm, k), lambda j, i: (i, 0)),
                  pl.BlockSpec((None, bn, k), lambda j, i: (layer, j, 0)),
                  pl.BlockSpec((None, 128, k), lambda j, i: (layer, (j + 1) * (bn // 128), 0)),
                  pl.BlockSpec((None, 1, bn), lambda j, i: (layer, 0, j))],
        out_specs=pl.BlockSpec((bm, bn), lambda j, i: (i, j)),
        out_shape=jax.ShapeDtypeStruct((m, N_MAIN), BF16),
        scratch_shapes=[pltpu.VMEM((bn, k), BF16)],
        compiler_params=_params(("arbitrary", "arbitrary")),
        name="in_proj",
    )(a, w_t, w_t, b_main.reshape(DEPTH, 1, N_MAIN))


def _gate_proj_kernel(a_ref, w1_ref, w2_ref, b_ref, o_ref, ws):
    @pl.when(pl.program_id(0) == 0)
    def _():
        ws[0:32, :] = w1_ref[0:32, :].astype(BF16)
        ws[32:64, :] = w2_ref[32:64, :].astype(BF16)
        ws[64:GATE_LANES, :] = jnp.zeros((GATE_LANES - 64, ws.shape[1]), BF16)
    o_ref[...] = _dot_nt(a_ref[...], ws[...]) + b_ref[...]


def _gate_proj(a, w_t, b_gate, layer):
    m, k = a.shape
    bm = 1024
    return pl.pallas_call(
        _gate_proj_kernel,
        grid=(m // bm,),
        in_specs=[pl.BlockSpec((bm, k), lambda i: (i, 0)),
                  pl.BlockSpec((None, 128, k), lambda i: (layer, 4 * W_MIX // 128, 0)),
                  pl.BlockSpec((None, 128, k), lambda i: (layer, 8 * W_MIX // 128, 0)),
                  pl.BlockSpec((None, 1, GATE_LANES), lambda i: (layer, 0, 0))],
        out_specs=pl.BlockSpec((bm, GATE_LANES), lambda i: (i, 0)),
        out_shape=jax.ShapeDtypeStruct((m, GATE_LANES), F32),
        scratch_shapes=[pltpu.VMEM((GATE_LANES, k), BF16)],
        compiler_params=_params(("arbitrary",)),
        name="gate_proj",
    )(a, w_t, w_t, b_gate.reshape(DEPTH, 1, GATE_LANES))


def _merge_kernel(ya_ref, yb_ref, yc_ref, w_ref, ga_ref, gb_ref, gc_ref, o_ref, ws):
    @pl.when(pl.program_id(1) == 0)
    def _():
        ws[...] = w_ref[...].astype(BF16)
    acc = _sigmoid(ga_ref[...].astype(F32)) * _dot(ya_ref[...], ws[0])
    acc = acc + _sigmoid(gb_ref[...].astype(F32)) * _dot(yb_ref[...], ws[1])
    acc = acc + _sigmoid(gc_ref[...].astype(F32)) * _dot(yc_ref[...], ws[2])
    o_ref[...] = acc.astype(o_ref.dtype)


def _merge(ya, yb, yc, w_br, layer, p):
    m = ya.shape[0]
    bm, bn = 512, 1024
    g0 = 12 * W_MIX // bn
    gstep = D_MODEL // bn
    yspec = pl.BlockSpec((bm, W_MIX), lambda j, i: (i, 0))
    return pl.pallas_call(
        _merge_kernel,
        grid=(D_MODEL // bn, m // bm),
        in_specs=[yspec, yspec, yspec,
                  pl.BlockSpec((None, 3, W_MIX, bn), lambda j, i: (layer, 0, 0, j)),
                  pl.BlockSpec((bm, bn), lambda j, i: (i, g0 + j)),
                  pl.BlockSpec((bm, bn), lambda j, i: (i, g0 + gstep + j)),
                  pl.BlockSpec((bm, bn), lambda j, i: (i, g0 + 2 * gstep + j))],
        out_specs=pl.BlockSpec((bm, bn), lambda j, i: (i, j)),
        out_shape=jax.ShapeDtypeStruct((m, D_MODEL), BF16),
        scratch_shapes=[pltpu.VMEM((3, W_MIX, bn), BF16)],
        compiler_params=_params(("arbitrary", "arbitrary")),
        name="branch_merge",
    )(ya, yb, yc, w_br, p, p, p)


def _gateprep_kernel(g_ref, pv_ref, gc_ref, grt_ref):
    bm = g_ref.shape[0]
    raw = g_ref[...]
    lane = lax.broadcasted_iota(jnp.int32, (bm, GATE_LANES), 1)
    row = lax.broadcasted_iota(jnp.int32, (bm, GATE_LANES), 0)
    fb = pv_ref[0:1, :]
    dtb = pv_ref[1:2, :]
    alog = pv_ref[2:3, :]
    lf = -_softplus(-(raw + fb))
    beta = _sigmoid(raw)
    gdec = -jnp.exp(alog) * _softplus(raw + dtb)
    act = jnp.where(lane < 16, raw, jnp.where(lane < 32, lf, jnp.where(lane < 48, beta, gdec)))
    seglen = jnp.where(lane < 32, CHUNK_A, CHUNK_B)
    rowmod = row & (seglen - 1)
    pre = act
    suf = act
    s = 1
    while s < max(CHUNK_A, CHUNK_B):
        pre = pre + jnp.where(rowmod >= s, pltpu.roll(pre, s, 0), 0.0)
        suf = suf + jnp.where(rowmod < seglen - s, pltpu.roll(suf, bm - s, 0), 0.0)
        s *= 2
    is_cum = ((lane >= 16) & (lane < 32)) | ((lane >= 48) & (lane < 64))
    backward = ((lane >> 3) & 1) == 1
    out = jnp.where(is_cum, jnp.where(backward, suf, pre), act)
    gc_ref[...] = out
    for t in range(bm // 128):
        grt_ref[:, t * 128:(t + 1) * 128] = out[t * 128:(t + 1) * 128, :].T


def _gateprep(graw, pv):
    m = graw.shape[0]
    bm = 512
    return pl.pallas_call(
        _gateprep_kernel,
        grid=(m // bm,),
        in_specs=[pl.BlockSpec((bm, GATE_LANES), lambda i: (i, 0)),
                  pl.BlockSpec((8, GATE_LANES), lambda i: (0, 0))],
        out_specs=[pl.BlockSpec((bm, GATE_LANES), lambda i: (i, 0)),
                   pl.BlockSpec((GATE_LANES, bm), lambda i: (0, i))],
        out_shape=[jax.ShapeDtypeStruct((m, GATE_LANES), F32),
                   jax.ShapeDtypeStruct((GATE_LANES, m), F32)],
        compiler_params=_params(("arbitrary",)),
        name="gate_prep",
    )(graw, pv)


def _gate_rows(grt, lane0, chunk, chunk_major=True):
    m = grt.shape[1]
    g = grt[lane0:lane0 + 32].reshape(4, N_HEADS, m // chunk, chunk)
    if chunk_major:
        return jnp.pad(jnp.transpose(g, (1, 2, 0, 3)), ((0, 0), (0, 0), (0, 4), (0, 0)))
    return jnp.pad(jnp.transpose(g, (1, 0, 2, 3)), ((0, 0), (0, 4), (0, 0), (0, 0)))


def _select_cols(gc, lane0):
    h = pl.program_id(1)
    src = lax.broadcasted_iota(jnp.int32, (GATE_LANES, GATE_LANES), 0)
    c = lax.broadcasted_iota(jnp.int32, (GATE_LANES, GATE_LANES), 1)
    sel = ((src == lane0 + 8 * c + h) & (c < 4)).astype(BF16)
    return _dot_exact_rhs01(gc, sel)


def _head_out(hsum, g, gate):
    y = hsum * lax.rsqrt(jnp.mean(hsum * hsum, axis=-1, keepdims=True) + EPS) * g
    return y * gate


def _tri_masks(n):
    i = lax.broadcasted_iota(jnp.int32, (n, n), 0)
    j = lax.broadcasted_iota(jnp.int32, (n, n), 1)
    return i, j


def _mlstm_kernel(nseg, has_init, *refs):
    L = CHUNK_A
    refs = list(refs)
    q_ref, k_ref, v_ref, o_ref, gc_ref, rows_ref, hng_ref = refs[:7]
    pos = 7
    if has_init:
        c0_ref, n0_ref, m0_ref = refs[pos:pos + 3]
        pos += 4
    (y_ref, cout_ref, nout_ref, mout_ref,
     hf_ref, hb_ref, gh_ref, sc_ref, kv_ref, ksum_ref, cp_ref, ce_ref) = refs[pos:]
    h_refs = (hf_ref, hb_ref)
    rb = q_ref.shape[0]
    nct = rb // L
    segc = nct // nseg
    head = pl.program_id(1)
    hd = HEAD_DIM
    scale = HEAD_DIM ** -0.5
    ones = jnp.ones((L, hd), BF16)
    gh_ref[...] = _select_cols(gc_ref[...], 0)
    i, j = _tri_masks(L)
    masks = (i >= j, i <= j)
    lanes = (nct, HEAD_DIM)
    for d in range(2):
        ig = rows_ref[d]
        b = rows_ref[2 + d]
        bl = b[:, L - 1:L] if d == 0 else b[:, 0:1]
        sc_ref[d, 0] = jnp.broadcast_to(bl, lanes)
        sc_ref[d, 1] = jnp.broadcast_to(jnp.max(bl - b + ig, axis=1, keepdims=True), lanes)

    def chunk_of(sg, c, d):
        return sg * segc + (c if d == 0 else segc - 1 - c)

    def scan_m(sg, carry):
        if has_init:
            ms = tuple(m0_ref[d, pl.ds(head, 1), :] for d in range(2))
        else:
            ms = (jnp.zeros((1, HEAD_DIM), F32),) * 2

        def step(c, ms):
            out = []
            for d in range(2):
                n = chunk_of(sg, c, d)
                sc_ref[d, 2, pl.ds(n, 1), :] = ms[d]
                m_new = jnp.maximum(sc_ref[d, 0, pl.ds(n, 1), :] + ms[d], sc_ref[d, 1, pl.ds(n, 1), :])
                sc_ref[d, 3, pl.ds(n, 1), :] = m_new
                out.append(m_new)
            return tuple(out)

        ms = lax.fori_loop(0, segc, step, ms)
        for d in range(2):
            mout_ref[sg, d:d + 1, :] = ms[d]
        return carry

    lax.fori_loop(0, nseg, scan_m, 0)

    def increments(g, carry):
        for u in range(MLSTM_GROUP):
            n = g * MLSTM_GROUP + u
            r = pl.multiple_of(n * L, L)
            k = k_ref[pl.ds(r, L), :].astype(F32)
            kt = k.T
            kb = k.astype(BF16)
            v1 = jnp.concatenate([v_ref[pl.ds(r, L), :].astype(BF16), ones], axis=1)
            for d in range(2):
                w_row = scale * jnp.exp(sc_ref[d, 0, pl.ds(n, 1), 0:1] - rows_ref[2 + d, pl.ds(n, 1), :]
                                        + rows_ref[d, pl.ds(n, 1), :] - sc_ref[d, 3, pl.ds(n, 1), 0:1])
                kv_ref[d, n] = _dot((kt * w_row).astype(BF16), v1)
                ksum_ref[d, pl.ds(n, 1), :] = _dot(jnp.broadcast_to(w_row, (8, L)).astype(BF16), kb)[0:1, :]
        return carry

    lax.fori_loop(0, nct // MLSTM_GROUP, increments, 0)

    def scan_state(sg, carry):
        if has_init:
            nrows = tuple(n0_ref[d, pl.ds(head, 1), :] for d in range(2))
            for d in range(2):
                ce_ref[d, :, 0:hd] = c0_ref[d]
                ce_ref[d, :, hd:2 * hd] = jnp.broadcast_to(nrows[d], (hd, hd)).T
        else:
            nrows = (jnp.zeros((1, hd), F32),) * 2
            ce_ref[...] = jnp.zeros((2, hd, 2 * hd), F32)

        def step(c, nrows):
            out = []
            for d in range(2):
                n = chunk_of(sg, c, d)
                ce = ce_ref[d]
                cp_ref[d, n] = ce.astype(BF16)
                w_s = jnp.exp(sc_ref[d, 0, pl.ds(n, 1), 0:1] + sc_ref[d, 2, pl.ds(n, 1), 0:1]
                              - sc_ref[d, 3, pl.ds(n, 1), 0:1])
                ce_ref[d] = w_s * ce + kv_ref[d, n]
                out.append(w_s * nrows[d] + ksum_ref[d, pl.ds(n, 1), :])
            return tuple(out)

        nrows = lax.fori_loop(0, segc, step, nrows)
        for d in range(2):
            cout_ref[sg, d] = ce_ref[d, :, 0:hd]
            nout_ref[sg, d:d + 1, :] = nrows[d]
        return carry

    lax.fori_loop(0, nseg, scan_state, 0)

    def outputs(g, carry):
        for u in range(MLSTM_GROUP):
            n = g * MLSTM_GROUP + u
            r = pl.multiple_of(n * L, L)
            qb = q_ref[pl.ds(r, L), :].astype(BF16)
            v1 = jnp.concatenate([v_ref[pl.ds(r, L), :].astype(BF16), ones], axis=1)
            qk = _dot_nt(qb, (k_ref[pl.ds(r, L), :].astype(F32) * scale).astype(BF16))
            gcol = gh_ref[pl.ds(r, L), :]
            for d in range(2):
                bcol = jnp.broadcast_to(gcol[:, 2 + d:3 + d], (L, L))
                m_prev = sc_ref[d, 2, pl.ds(n, 1), 0:1]
                logd = jnp.where(masks[d], bcol - rows_ref[2 + d, pl.ds(n, 1), :] + rows_ref[d, pl.ds(n, 1), :],
                                 NEG_BIG)
                m_i = jnp.maximum(bcol + m_prev, jnp.broadcast_to(jnp.max(logd, axis=1, keepdims=True), (L, L)))
                s = qk * jnp.exp(logd - m_i)
                w_prev = jnp.exp(bcol + m_prev - m_i)
                qc = _dot(qb, cp_ref[d, n])
                sv = _dot(s.astype(BF16), v1)
                num = w_prev * qc[:, 0:hd] + sv[:, 0:hd]
                den = w_prev * qc[:, hd:2 * hd] + sv[:, hd:2 * hd]
                h_refs[d][pl.ds(r, L), :] = num / jnp.maximum(jnp.abs(den), jnp.exp(-m_i))
        return carry

    lax.fori_loop(0, nct // MLSTM_GROUP, outputs, 0)
    y = _head_out(hf_ref[...] + hb_ref[...], hng_ref[...], _sigmoid(o_ref[...].astype(F32)))
    y_ref[...] = y.astype(y_ref.dtype)


def _ret_kernel(nseg, has_init, use_rope, *refs):
    L = CHUNK_C
    refs = list(refs)
    q_ref, k_ref, v_ref, g_ref, lg_ref, hng_ref = refs[:6]
    pos = 6
    if use_rope:
        cos_ref, sin_ref = refs[pos:pos + 2]
        pos += 2
    if has_init:
        s0_ref = refs[pos]
        pos += 2
    y_ref, sout_ref, o_ref, qs_ref, ks_ref, kv_ref, sp_ref = refs[pos:]
    rb = q_ref.shape[0]
    nct = rb // L
    segc = nct // nseg
    hd = HEAD_DIM

    q = q_ref[...].astype(F32) * HEAD_DIM ** -0.5
    k = k_ref[...].astype(F32)
    if use_rope:
        lane = lax.broadcasted_iota(jnp.int32, (rb, HEAD_DIM), 1)
        cos = cos_ref[...]
        sin = sin_ref[...]
        half = HEAD_DIM // 2

        def rope(x):
            rot = pltpu.roll(x, half, 1)
            return x * cos + jnp.where(lane < half, -rot, rot) * sin
        q = rope(q)
        k = rope(k)
    qs_ref[...] = q
    ks_ref[...] = k

    lg_f = lg_ref[0:1, 0:1]
    lg_b = lg_ref[1:2, 0:1]
    i, j = _tri_masks(L)
    dij = (i - j).astype(F32)
    decay_f = jnp.where(i >= j, jnp.exp(lg_f * jnp.maximum(dij, 0.0)), 0.0)
    decay_b = jnp.where(i <= j, jnp.exp(lg_b * jnp.maximum(-dij, 0.0)), 0.0)
    pos_c = lax.broadcasted_iota(jnp.int32, (L, 1), 0).astype(F32)
    qdec_f = jnp.exp(lg_f * (pos_c + 1.0))
    kend_f = jnp.exp(lg_f * (L - 1.0 - pos_c))
    qdec_b = jnp.exp(lg_b * (L - pos_c))
    kend_b = jnp.exp(lg_b * pos_c)
    cdec_f = jnp.exp(lg_f * L)
    cdec_b = jnp.exp(lg_b * L)

    decay = decay_f + decay_b

    def increments(g, carry):
        for u in range(RET_GROUP):
            n = g * RET_GROUP + u
            r = pl.multiple_of(n * L, L)
            kc = ks_ref[pl.ds(r, L), :]
            ke = jnp.concatenate([kc * kend_f, kc * kend_b], axis=1).astype(BF16)
            kv_ref[n] = _dot_tn(ke, v_ref[pl.ds(r, L), :].astype(BF16))
        return carry

    lax.fori_loop(0, nct // RET_GROUP, increments, 0)

    def scan_state(sg, carry):
        if has_init:
            st = (s0_ref[0], s0_ref[1])
        else:
            st = (jnp.zeros((hd, hd), F32),) * 2

        def step(c, st):
            nf = sg * segc + c
            nb = sg * segc + (segc - 1 - c)
            sp_ref[nf, 0:hd, :] = st[0].astype(BF16)
            sp_ref[nb, hd:2 * hd, :] = st[1].astype(BF16)
            return (cdec_f * st[0] + kv_ref[nf, 0:hd, :], cdec_b * st[1] + kv_ref[nb, hd:2 * hd, :])

        st = lax.fori_loop(0, segc, step, st)
        sout_ref[sg, 0] = st[0]
        sout_ref[sg, 1] = st[1]
        return carry

    lax.fori_loop(0, nseg, scan_state, 0)

    def outputs(g, carry):
        for u in range(RET_GROUP):
            n = g * RET_GROUP + u
            r = pl.multiple_of(n * L, L)
            qc = qs_ref[pl.ds(r, L), :]
            qk = _dot_nt(qc.astype(BF16), ks_ref[pl.ds(r, L), :].astype(BF16))
            inner = _dot((qk * decay).astype(BF16), v_ref[pl.ds(r, L), :].astype(BF16))
            qd = jnp.concatenate([qc * qdec_f, qc * qdec_b], axis=1).astype(BF16)
            o_ref[pl.ds(r, L), :] = inner + _dot(qd, sp_ref[n])
        return carry

    lax.fori_loop(0, nct // RET_GROUP, outputs, 0)
    gate = g_ref[...].astype(F32)
    y = _head_out(o_ref[...], hng_ref[...], gate * _sigmoid(gate))
    y_ref[...] = y.astype(y_ref.dtype)


def _retention(p, lgrows, hng, rope, y_prev, s0, *, blk0, nblk, nseg, layer, col0, rb=ROW_GROUP):
    m = p.shape[0]
    assert (rb // CHUNK_C) % RET_GROUP == 0
    colspec = lambda c0: pl.BlockSpec((rb, HEAD_DIM), lambda b, h: (blk0 + b, c0 + h))
    in_specs = [colspec(col0), colspec(col0 + 8), colspec(col0 + 16), colspec(col0 + 24),
                pl.BlockSpec((None, 8, HEAD_DIM), lambda b, h: (h, 0, 0)),
                pl.BlockSpec((1, HEAD_DIM), lambda b, h: (0, 0))]
    args = [p, p, p, p, lgrows, hng]
    if rope is not None:
        in_specs += [pl.BlockSpec((rb, HEAD_DIM), lambda b, h: (0, 0))] * 2
        args += list(rope)
    aliases = {}
    if s0 is not None:
        in_specs += [pl.BlockSpec((None, None, 2, None, HEAD_DIM, HEAD_DIM), lambda b, h: (b, layer, 0, h, 0, 0)),
                     pl.BlockSpec(memory_space=pl.ANY)]
        args += [s0, y_prev]
        aliases = {len(args) - 1: 0}
    nsq = nblk * nseg
    return pl.pallas_call(
        functools.partial(_ret_kernel, nseg, s0 is not None, rope is not None),
        grid=(nblk, N_HEADS),
        in_specs=in_specs,
        out_specs=[pl.BlockSpec((rb, HEAD_DIM), lambda b, h: (blk0 + b, h)),
                   pl.BlockSpec((nseg, 2, None, HEAD_DIM, HEAD_DIM), lambda b, h: (b, 0, h, 0, 0))],
        out_shape=[jax.ShapeDtypeStruct((m, W_MIX), BF16),
                   jax.ShapeDtypeStruct((nsq, 2, N_HEADS, HEAD_DIM, HEAD_DIM), F32)],
        scratch_shapes=[pltpu.VMEM((rb, HEAD_DIM), F32), pltpu.VMEM((rb, HEAD_DIM), F32),
                        pltpu.VMEM((rb, HEAD_DIM), F32),
                        pltpu.VMEM((rb // CHUNK_C, 2 * HEAD_DIM, HEAD_DIM), F32),
                        pltpu.VMEM((rb // CHUNK_C, 2 * HEAD_DIM, HEAD_DIM), BF16)],
        input_output_aliases=aliases,
        compiler_params=_params(("arbitrary", "arbitrary")),
        name="retention",
    )(*args)


def _inv_unit_triangular(mats, eye, i, j):
    n = eye.shape[0]
    xs = [eye - jnp.where((i >> 1) == (j >> 1), a, 0.0) for a in mats]
    s, sh = 2, 1
    while s < n:
        join = ((i >> (sh + 1)) == (j >> (sh + 1))) & ((i >> sh) != (j >> sh))
        xb = [x.astype(BF16) for x in xs]
        mt = [_dot(jnp.where(join, a, 0.0).astype(BF16), b).astype(BF16) for a, b in zip(mats, xb)]
        xs = [x - _dot(b, t) for x, b, t in zip(xs, xb, mt)]
        s, sh = 2 * s, sh + 1
    rs = [(eye - x - _dot_x3(a, x)).astype(BF16) for a, x in zip(mats, xs)]
    return [x + _dot(x.astype(BF16), r) for x, r in zip(xs, rs)]


def _delta_kernel(nseg, has_init, *refs):
    L = CHUNK_B
    refs = list(refs)
    q_ref, k_ref, v_ref, z_ref, cwq_ref, cwk_ref, cwv_ref, gc_ref, rows_ref, hng_ref = refs[:10]
    pos = 10
    if has_init:
        s0_ref = refs[pos]
        pos += 2
    (y_ref, sout_ref, qs_ref, ks_ref, vs_ref, gh_ref, u0f_ref, u0b_ref, wkf_ref, wkb_ref,
     qkf_ref, qkb_ref, qdf_ref, qdb_ref, ketf_ref, ketb_ref, sf_ref, sb_ref) = refs[pos:]
    of_ref, ob_ref = qs_ref, vs_ref
    rb = q_ref.shape[0]
    tseg = rb // nseg
    segc = tseg // L
    nchunks = rb // L

    row = lax.broadcasted_iota(jnp.int32, (rb, HEAD_DIM), 0)
    rowmod = row % tseg
    first = rowmod == 0
    last = rowmod == tseg - 1

    def conv_silu(x_ref, w_ref):
        x = x_ref[...].astype(F32)
        prev = jnp.where(first, 0.0, pltpu.roll(x, 1, 0))
        nxt = jnp.where(last, 0.0, pltpu.roll(x, rb - 1, 0))
        y = w_ref[0:1, :] * prev + w_ref[1:2, :] * x + w_ref[2:3, :] * nxt
        return y * _sigmoid(y)

    def l2n(x):
        return x * lax.rsqrt(jnp.sum(x * x, axis=-1, keepdims=True) + EPS)

    qs_ref[...] = l2n(conv_silu(q_ref, cwq_ref)) * HEAD_DIM ** -0.5
    ks_ref[...] = l2n(conv_silu(k_ref, cwk_ref))
    vs_ref[...] = conv_silu(v_ref, cwv_ref)
    gh_ref[...] = _select_cols(gc_ref[...], 32)

    i, j = _tri_masks(L)
    eye = (i == j).astype(F32)
    masks = ((i >= j, i > j), (i <= j, i < j))

    dir_refs = ((u0f_ref, wkf_ref, qkf_ref, qdf_ref, ketf_ref), (u0b_ref, wkb_ref, qkb_ref, qdb_ref, ketb_ref))

    def g_last(rw, d):
        return rw[2:3, L - 1:L] if d == 0 else rw[3:4, 0:1]

    def prep(gi, carry):
        chains = []
        for u in range(PREP_GROUP):
            c = gi * PREP_GROUP + u
            r = pl.multiple_of(c * L, L)
            qc = qs_ref[pl.ds(r, L), :]
            kc = ks_ref[pl.ds(r, L), :]
            kb = kc.astype(BF16)
            kk = _dot_nt(kb, kb)
            qk = _dot_nt(qc.astype(BF16), kb)
            gcol = gh_ref[pl.ds(r, L), :]
            rw = rows_ref[c]
            for d in range(2):
                incl, strict = masks[d]
                g_c = gcol[:, 2 + d:3 + d]
                decay = jnp.where(incl, jnp.exp(jnp.where(incl, g_c - rw[2 + d:3 + d, :], 0.0)), 0.0)
                dir_refs[d][2][pl.ds(r, L), :] = (qk * decay).astype(BF16)
                dir_refs[d][3][pl.ds(r, L), :] = (qc * jnp.exp(g_c)).astype(BF16)
                dir_refs[d][4][c] = (kc * jnp.exp(g_last(rw, d) - g_c)).T.astype(BF16)
                chains.append((r, d, jnp.where(strict, kk * decay, 0.0) * gcol[:, d:d + 1]))
        invs = _inv_unit_triangular([a for _, _, a in chains], eye, i, j)
        for (r, d, _), t in zip(chains, invs):
            gcol = gh_ref[pl.ds(r, L), :]
            beta_c = gcol[:, d:d + 1]
            bg = beta_c * jnp.exp(gcol[:, 2 + d:3 + d])
            tb = t.astype(BF16)
            dir_refs[d][0][pl.ds(r, L), :] = _dot(tb, (beta_c * vs_ref[pl.ds(r, L), :]).astype(BF16))
            dir_refs[d][1][pl.ds(r, L), :] = _dot(tb, (bg * ks_ref[pl.ds(r, L), :]).astype(BF16)).astype(BF16)
        return carry

    lax.fori_loop(0, nchunks // PREP_GROUP, prep, 0)

    def one(c, d, s_ref, o_ref):
        u0_ref, wk_ref, qkd_ref, qd_ref, ket_ref = dir_refs[d]
        r = pl.multiple_of(c * L, L)
        s_prev = s_ref[...]
        sb = s_prev.astype(BF16)
        ub = (u0_ref[pl.ds(r, L), :] - _dot(wk_ref[pl.ds(r, L), :], sb)).astype(BF16)
        o_ref[pl.ds(r, L), :] = _dot(qd_ref[pl.ds(r, L), :], sb) + _dot(qkd_ref[pl.ds(r, L), :], ub)
        s_ref[...] = jnp.exp(g_last(rows_ref[c], d)) * s_prev + _dot(ket_ref[c], ub)

    def segment(sg, carry):
        if has_init:
            sf_ref[...] = s0_ref[0]
            sb_ref[...] = s0_ref[1]
        else:
            sf_ref[...] = jnp.zeros((HEAD_DIM, HEAD_DIM), F32)
            sb_ref[...] = jnp.zeros((HEAD_DIM, HEAD_DIM), F32)

        def step(c, carry2):
            one(sg * segc + c, 0, sf_ref, of_ref)
            one(sg * segc + (segc - 1 - c), 1, sb_ref, ob_ref)
            return carry2

        lax.fori_loop(0, segc, step, 0)
        sout_ref[sg, 0] = sf_ref[...]
        sout_ref[sg, 1] = sb_ref[...]
        return carry

    lax.fori_loop(0, nseg, segment, 0)
    gate = z_ref[...].astype(F32)
    y = _head_out(of_ref[...] + ob_ref[...], hng_ref[...], gate * _sigmoid(gate))
    y_ref[...] = y.astype(y_ref.dtype)


def _delta(p, conv_w, gc, rows, hng, y_prev, s0, *, blk0, nblk, nseg, layer, col0, rb=ROW_GROUP):
    m = p.shape[0]
    cb = rb // CHUNK_B
    assert cb % PREP_GROUP == 0
    colspec = lambda c0: pl.BlockSpec((rb, HEAD_DIM), lambda b, h: (blk0 + b, c0 + h))
    cwspec = lambda c0: pl.BlockSpec((None, CONV_K, HEAD_DIM), lambda b, h: (layer, 0, c0 + h))
    in_specs = [colspec(col0), colspec(col0 + 8), colspec(col0 + 16), colspec(col0 + 24),
                cwspec(0), cwspec(8), cwspec(16),
                pl.BlockSpec((rb, GATE_LANES), lambda b, h: (blk0 + b, 0)),
                pl.BlockSpec((None, cb, 8, CHUNK_B), lambda b, h: (h, blk0 + b, 0, 0)),
                pl.BlockSpec((1, HEAD_DIM), lambda b, h: (0, 0))]
    args = [p, p, p, p, conv_w, conv_w, conv_w, gc, rows, hng]
    aliases = {}
    if s0 is not None:
        in_specs += [pl.BlockSpec((None, None, 2, None, HEAD_DIM, HEAD_DIM), lambda b, h: (b, layer, 0, h, 0, 0)),
                     pl.BlockSpec(memory_space=pl.ANY)]
        args += [s0, y_prev]
        aliases = {len(args) - 1: 0}
    nsq = nblk * nseg
    big = lambda dt: pltpu.VMEM((rb, HEAD_DIM), dt)
    return pl.pallas_call(
        functools.partial(_delta_kernel, nseg, s0 is not None),
        grid=(nblk, N_HEADS),
        in_specs=in_specs,
        out_specs=[pl.BlockSpec((rb, HEAD_DIM), lambda b, h: (blk0 + b, h)),
                   pl.BlockSpec((nseg, 2, None, HEAD_DIM, HEAD_DIM), lambda b, h: (b, 0, h, 0, 0))],
        out_shape=[jax.ShapeDtypeStruct((m, W_MIX), BF16),
                   jax.ShapeDtypeStruct((nsq, 2, N_HEADS, HEAD_DIM, HEAD_DIM), F32)],
        scratch_shapes=[big(F32), big(F32), big(F32), pltpu.VMEM((rb, GATE_LANES), F32),
                        big(F32), big(F32), big(BF16), big(BF16),
                        pltpu.VMEM((rb, CHUNK_B), BF16), pltpu.VMEM((rb, CHUNK_B), BF16),
                        big(BF16), big(BF16),
                        pltpu.VMEM((cb, HEAD_DIM, CHUNK_B), BF16), pltpu.VMEM((cb, HEAD_DIM, CHUNK_B), BF16),
                        pltpu.VMEM((HEAD_DIM, HEAD_DIM), F32), pltpu.VMEM((HEAD_DIM, HEAD_DIM), F32)],
        input_output_aliases=aliases,
        compiler_params=_params(("arbitrary", "arbitrary")),
        name="deltanet",
    )(*args)


def _state_specs(nseg):
    specs = [pl.BlockSpec((nseg, 2, None, HEAD_DIM, HEAD_DIM), lambda b, h: (b, 0, h, 0, 0)),
             pl.BlockSpec((None, nseg, 2, HEAD_DIM), lambda b, h: (h, b, 0, 0)),
             pl.BlockSpec((None, nseg, 2, HEAD_DIM), lambda b, h: (h, b, 0, 0))]
    return specs


def _mlstm(p, gc, rows, hng, y_prev, init, *, blk0, nblk, nseg, layer, rb=ROW_GROUP):
    m = p.shape[0]
    cb = rb // CHUNK_A
    assert cb % MLSTM_GROUP == 0
    colspec = lambda c0: pl.BlockSpec((rb, HEAD_DIM), lambda b, h: (blk0 + b, c0 + h))
    in_specs = [colspec(0), colspec(8), colspec(16), colspec(24),
                pl.BlockSpec((rb, GATE_LANES), lambda b, h: (blk0 + b, 0)),
                pl.BlockSpec((None, 8, cb, CHUNK_A), lambda b, h: (h, 0, blk0 + b, 0)),
                pl.BlockSpec((1, HEAD_DIM), lambda b, h: (0, 0))]
    args = [p, p, p, p, gc, rows, hng]
    if init is not None:
        c0, n0, m0 = init
        in_specs += [pl.BlockSpec((None, None, 2, None, HEAD_DIM, HEAD_DIM), lambda b, h: (b, layer, 0, h, 0, 0)),
                     pl.BlockSpec((None, None, 2, N_HEADS, HEAD_DIM), lambda b, h: (b, layer, 0, 0, 0)),
                     pl.BlockSpec((None, None, 2, N_HEADS, HEAD_DIM), lambda b, h: (b, layer, 0, 0, 0)),
                     pl.BlockSpec(memory_space=pl.ANY)]
        args += [c0, n0, m0, y_prev]
        aliases = {len(args) - 1: 0}
    else:
        aliases = {}
    nsq = nblk * nseg
    out_shape = [jax.ShapeDtypeStruct((m, W_MIX), BF16),
                 jax.ShapeDtypeStruct((nsq, 2, N_HEADS, HEAD_DIM, HEAD_DIM), F32),
                 jax.ShapeDtypeStruct((N_HEADS, nsq, 2, HEAD_DIM), F32),
                 jax.ShapeDtypeStruct((N_HEADS, nsq, 2, HEAD_DIM), F32)]
    out_specs = [pl.BlockSpec((rb, HEAD_DIM), lambda b, h: (blk0 + b, h))] + _state_specs(nseg)
    return pl.pallas_call(
        functools.partial(_mlstm_kernel, nseg, init is not None),
        grid=(nblk, N_HEADS),
        in_specs=in_specs, out_specs=out_specs, out_shape=out_shape,
        scratch_shapes=[pltpu.VMEM((rb, HEAD_DIM), F32), pltpu.VMEM((rb, HEAD_DIM), F32),
                        pltpu.VMEM((rb, GATE_LANES), F32),
                        pltpu.VMEM((2, 4, cb, HEAD_DIM), F32),
                        pltpu.VMEM((2, cb, HEAD_DIM, 2 * HEAD_DIM), F32), pltpu.VMEM((2, cb, HEAD_DIM), F32),
                        pltpu.VMEM((2, cb, HEAD_DIM, 2 * HEAD_DIM), BF16),
                        pltpu.VMEM((2, HEAD_DIM, 2 * HEAD_DIM), F32)],
        input_output_aliases=aliases,
        compiler_params=_params(("arbitrary", "arbitrary")),
        name="mlstm",
    )(*args)


_MAIN_COLS = ((0, 4 * W_MIX), (4 * W_MIX + 32, 8 * W_MIX + 32), (8 * W_MIX + 64, 12 * W_MIX + 64 + 3 * D_MODEL))
_GATE_COLS = ((4 * W_MIX, 4 * W_MIX + 32), (8 * W_MIX + 32, 8 * W_MIX + 64))
COL_A, COL_B, COL_C = 0, 32, 64


def _take_cols(w, ranges):
    return jnp.concatenate([w[..., a:b] for a, b in ranges], axis=-1)


def _rope_tables(t):
    pos = jnp.arange(t)
    pos_r = (pos // GRID_W).astype(F32)
    pos_c = (pos % GRID_W).astype(F32)
    nf = HEAD_DIM // 4
    freqs = ROPE_BASE ** (-jnp.arange(nf, dtype=F32) / nf)
    ang = jnp.concatenate([pos_r[:, None] * freqs, pos_c[:, None] * freqs], axis=-1)
    ang = jnp.concatenate([ang, ang], axis=-1)
    return jnp.cos(ang), jnp.sin(ang)


def kernel(x_prompt, x_sample, state_mlstm_C, state_mlstm_n, state_mlstm_m, state_delta_S, state_ret_S,
           c, c_ctx, norm_g, final_norm_g, w_ada, b_ada, w_in, b_in, mlstm_f_bias, conv_w, delta_A_log,
           delta_dt_bias, ret_log_gamma, head_norm_g, w_br, w_out, ffn_w13, ffn_w2):
    bp, sp, d = x_prompt.shape
    bs, ss, _ = x_sample.shape
    assert d == D_MODEL and bp * sp == ROW_GROUP and ss == ROW_GROUP
    n_prompt_blk, n_sample_blk = 1, bs
    x = jnp.concatenate([x_prompt.reshape(-1, d), x_sample.reshape(-1, d)], axis=0)

    cv8 = jnp.zeros((8, d), F32).at[0].set(c_ctx).at[1:1 + bs].set(c)
    mod = _ada(cv8, w_ada, b_ada)[:, :1 + bs].reshape(DEPTH, 1 + bs, N_MOD, 1, d)
    rope = _rope_tables(ss)
    m0_pad = jnp.broadcast_to(state_mlstm_m[..., None], state_mlstm_m.shape + (HEAD_DIM,))

    w_t = jnp.swapaxes(w_in, 1, 2)
    b_main = _take_cols(b_in, _MAIN_COLS)
    pad = GATE_LANES - 64
    b_gate = jnp.pad(_take_cols(b_in, _GATE_COLS), ((0, 0), (0, pad)))

    new_c, new_n, new_m, new_sd, new_sr = [], [], [], [], []
    for l in range(DEPTH):
        md = lambda i: mod[l, :, i]
        h = _normmod(x, norm_g[l, 0], md(0), md(1))
        x = _mm_res(_mm_swiglu(h, ffn_w13, (l, 0)), ffn_w2, (l, 0), x, md(2), 0.5, 1024, 512)

        h = _normmod(x, norm_g[l, 1], md(3), md(4))
        p = _in_proj(h, w_t, b_main, l)
        graw = _gate_proj(h, w_t, b_gate, l)
        pv = jnp.zeros((8, GATE_LANES), F32)
        pv = pv.at[0, 16:32].set(mlstm_f_bias[l].reshape(-1))
        pv = pv.at[1, 48:64].set(delta_dt_bias[l].reshape(-1))
        pv = pv.at[2, 48:64].set(delta_A_log[l].reshape(-1))
        gc, grt = _gateprep(graw, pv)
        rows_a = _gate_rows(grt, 0, CHUNK_A, chunk_major=False)
        rows_b = _gate_rows(grt, 32, CHUNK_B)
        hng = head_norm_g[l].reshape(3, 1, HEAD_DIM)
        lgrows = jnp.zeros((N_HEADS, 8, HEAD_DIM), F32).at[:, 0:2, :].set(
            jnp.broadcast_to(ret_log_gamma[l].T[:, :, None], (N_HEADS, 2, HEAD_DIM)))
        prompt = dict(blk0=0, nblk=n_prompt_blk, nseg=bp, layer=l)
        sample = dict(blk0=n_prompt_blk, nblk=n_sample_blk, nseg=1, layer=l)

        ya, c_new, n_new, m_new = _mlstm(p, gc, rows_a, hng[0], None, None, **prompt)
        ya = _mlstm(p, gc, rows_a, hng[0], ya, (state_mlstm_C, state_mlstm_n, m0_pad), **sample)[0]
        yb, sd_new = _delta(p, conv_w, gc, rows_b, hng[1], None, None, col0=COL_B, **prompt)
        yb = _delta(p, conv_w, gc, rows_b, hng[1], yb, state_delta_S, col0=COL_B, **sample)[0]
        yc, sr_new = _retention(p, lgrows, hng[2], None, None, None, col0=COL_C, **prompt)
        yc = _retention(p, lgrows, hng[2], rope, yc, state_ret_S, col0=COL_C, **sample)[0]
        new_c.append(c_new)
        new_n.append(jnp.transpose(n_new, (1, 2, 0, 3)))
        new_m.append(jnp.transpose(m_new[..., 0], (1, 2, 0)))
        new_sd.append(sd_new)
        new_sr.append(sr_new)

        x = _mm_res(_merge(ya, yb, yc, w_br, l, p), w_out, (l,), x, md(5), 1.0, 1024, 1024)
        h = _normmod(x, norm_g[l, 2], md(6), md(7))
        x = _mm_res(_mm_swiglu(h, ffn_w13, (l, 1)), ffn_w2, (l, 1), x, md(8), 0.5, 1024, 512)

    y_prompt = _final_norm(x, final_norm_g, 0, bp * sp).reshape(bp, sp, d)
    y_sample = _final_norm(x, final_norm_g, bp * sp, bs * ss).reshape(bs, ss, d)
    stack = lambda xs: jnp.stack(xs, axis=1)
    return (y_prompt, y_sample, stack(new_c), stack(new_n), stack(new_m), stack(new_sd), stack(new_sr))
```
